```python
import math
import jax, jax.numpy as jnp
from jax import lax
import numpy as np

D_MODEL = 4096
BATCH = 4
SEQ = 2048
DEPTH = 4
DEC_BATCH = 128
DEC_SEQ = 8
PAST_LEN = 16384
PAGE_SIZE = 128

N_MIXERS = 2
N_A_LAYERS = (DEPTH + 1) // 2
N_B_LAYERS = DEPTH // 2
CHUNK = 128
GM_HALF = 2 * D_MODEL
GM_GROUPS = 8
GM_GDIM = GM_HALF // GM_GROUPS
SSM_WIDTH = D_MODEL
SSM_GCH = 16
SSM_GROUPS = SSM_WIDTH // SSM_GCH
SSM_STATE = 64
DT_MIN = 1e-3
DT_MAX = 1e-1
PEER_KEYS = 128
PEER_EXPERTS = PEER_KEYS * PEER_KEYS
PEER_HEADS = 8
PEER_TOPK = 16
PEER_DKEY = 128
PEER_BLOCK = 128
EPS = 1e-6

kernel_name = "hybrid_gmlp_s5_peer_adaln_decode_step"


def rmsnorm(x, g):
    xf = x.astype(jnp.float32)
    var = jnp.mean(xf * xf, axis=-1, keepdims=True)
    return (xf * lax.rsqrt(var + EPS)).astype(x.dtype) * g


def ada_mod(c, w, b):
    m = (jax.nn.silu(c) @ w + b)[:, None, :]
    return jnp.split(m, 6, axis=-1)


def gmlp_mixer(h, blk, w_in, b_in, v_g, w_s, b_s, w_out):
    Bn, L, _ = h.shape
    z = jax.nn.gelu(h @ w_in + b_in)
    u, v = jnp.split(z, 2, axis=-1)
    v = rmsnorm(v, v_g)
    nc = L // blk
    vg = v.reshape(Bn, nc, blk, GM_GROUPS, GM_GDIM)
    mask = jnp.tril(jnp.ones((blk, blk), dtype=bool))
    w = jnp.where(mask, w_s[:, :blk, :blk], 0)
    bias = jnp.transpose(b_s[:, :blk])[None, None, :, :, None]
    s = jnp.einsum('gts,bcsgd->bctgd', w, vg) + bias
    out = (u * s.reshape(Bn, L, GM_HALF)) @ w_out
    return out, v


def ssm_discretize(lam_re, lam_im, log_dt, b_re, b_im):
    dt = jnp.exp(log_dt)[:, None]
    mag = jnp.exp(lam_re * dt)
    ang = lam_im * dt
    a_re = mag * jnp.cos(ang)
    a_im = mag * jnp.sin(ang)
    n_re = a_re - 1
    den = lam_re * lam_re + lam_im * lam_im
    f_re = (n_re * lam_re + a_im * lam_im) / den
    f_im = (a_im * lam_re - n_re * lam_im) / den
    bb_re = f_re[..., None] * b_re - f_im[..., None] * b_im
    bb_im = f_re[..., None] * b_im + f_im[..., None] * b_re
    return a_re, a_im, bb_re, bb_im


def _ssm_combine(e1, e2):
    a1r, a1i, b1r, b1i = e1
    a2r, a2i, b2r, b2i = e2
    return (a2r * a1r - a2i * a1i,
            a2r * a1i + a2i * a1r,
            a2r * b1r - a2i * b1i + b2r,
            a2r * b1i + a2i * b1r + b2i)


def ssm_mixer(h, h0_re, h0_im, blk, w_in, lam_re, lam_im, log_dt, b_re, b_im,
              c_re, c_im, d_skip, w_out):
    Bn, L, _ = h.shape
    xs = h @ w_in
    a_re, a_im, bb_re, bb_im = ssm_discretize(lam_re, lam_im, log_dt, b_re, b_im)
    nb = L // blk
    xg = xs.reshape(Bn, nb, blk, SSM_GROUPS, SSM_GCH).swapaxes(0, 1)

    def step(carry, xb):
        hr, hi = carry
        bur = jnp.einsum('gpk,blgk->blgp', bb_re, xb)
        bui = jnp.einsum('gpk,blgk->blgp', bb_im, xb)
        ar = jnp.broadcast_to(a_re, bur.shape)
        ai = jnp.broadcast_to(a_im, bur.shape)
        acr, aci, sr, si = lax.associative_scan(_ssm_combine, (ar, ai, bur, bui), axis=1)
        str_ = acr * hr[:, None] - aci * hi[:, None] + sr
        sti = acr * hi[:, None] + aci * hr[:, None] + si
        y = (jnp.einsum('gkp,blgp->blgk', c_re, str_)
             - jnp.einsum('gkp,blgp->blgk', c_im, sti))
        return (str_[:, -1].astype(hr.dtype), sti[:, -1].astype(hi.dtype)), y

    (hr, hi), ys = lax.scan(step, (h0_re, h0_im), xg)
    y = ys.swapaxes(0, 1).reshape(Bn, L, SSM_WIDTH) + d_skip * xs
    y = jax.nn.gelu(y)
    a, g = jnp.split(y @ w_out, 2, axis=-1)
    return a * jax.nn.sigmoid(g), hr, hi


def peer_ffn(h, w_q, keys, u_tab, v_tab):
    Bn, L, D = h.shape
    x = h.reshape(-1, D)
    T = x.shape[0]
    nblk = -(-T // PEER_BLOCK)
    xp = jnp.pad(x, ((0, nblk * PEER_BLOCK - T), (0, 0))).reshape(nblk, PEER_BLOCK, D)

    def block_fn(xb):
        q = (xb @ w_q).reshape(PEER_BLOCK, PEER_HEADS, 2, PEER_DKEY)
        s = jnp.einsum('thzk,hznk->thzn', q, keys)
        sv, si = lax.top_k(s, PEER_TOPK)
        cand = sv[:, :, 0, :, None] + sv[:, :, 1, None, :]
        cidx = si[:, :, 0, :, None] * PEER_KEYS + si[:, :, 1, None, :]
        cand = cand.reshape(PEER_BLOCK, PEER_HEADS, PEER_TOPK * PEER_TOPK)
        cidx = cidx.reshape(PEER_BLOCK, PEER_HEADS, PEER_TOPK * PEER_TOPK)
        fv, fi = lax.top_k(cand, PEER_TOPK)
        eidx = jnp.take_along_axis(cidx, fi, axis=-1)
        gate = jax.nn.softmax(fv.astype(jnp.float32), axis=-1).astype(xb.dtype)
        ue = jnp.take(u_tab, eidx, axis=0)
        ve = jnp.take(v_tab, eidx, axis=0)
        act = jax.nn.gelu(jnp.einsum('thkd,td->thk', ue, xb)) * gate
        return jnp.einsum('thk,thkd->td', act, ve)

    y = lax.map(block_fn, xp).reshape(-1, D)[:T]
    return y.reshape(Bn, L, D)


def trunk(x, c, h0_re, h0_im, blk, p):
    new_v, new_re, new_im = [], [], []
    ia = 0
    ib = 0
    for i in range(DEPTH):
        sh1, sc1, g1, sh2, sc2, g2 = ada_mod(c, p['ada_w'][i], p['ada_b'][i])
        h = rmsnorm(x, p['norm_mix_g'][i]) * (1 + sc1) + sh1
        if i % N_MIXERS == 0:
            out, v = gmlp_mixer(h, blk, p['gm_w_in'][ia], p['gm_b_in'][ia], p['gm_v_g'][ia],
                                p['gm_w_s'][ia], p['gm_b_s'][ia], p['gm_w_out'][ia])
            new_v.append(v)
            ia += 1
        else:
            out, hr, hi = ssm_mixer(h, h0_re[ib], h0_im[ib], blk, p['ssm_w_in'][ib],
                                    p['ssm_lam_re'][ib], p['ssm_lam_im'][ib], p['ssm_log_dt'][ib],
                                    p['ssm_b_re'][ib], p['ssm_b_im'][ib], p['ssm_c_re'][ib],
                                    p['ssm_c_im'][ib], p['ssm_d'][ib], p['ssm_w_out'][ib])
            new_re.append(hr)
            new_im.append(hi)
            ib += 1
        x = x + g1 * out
        h = rmsnorm(x, p['norm_ffn_g'][i]) * (1 + sc2) + sh2
        x = x + g2 * peer_ffn(h, p['peer_w_q'][i], p['peer_keys'][i], p['peer_u'][i], p['peer_v'][i])
    y = rmsnorm(x, p['final_g'])
    return y, jnp.stack(new_v), jnp.stack(new_re), jnp.stack(new_im)


def setup_inputs(seed: int = 0) -> dict:
    key = jax.random.key(seed)
    ks = iter(jax.random.split(key, 48))

    def nrm(shape, scale):
        return jax.random.normal(next(ks), shape, jnp.float32) * scale

    D = D_MODEL
    ssm_sh = (N_B_LAYERS, SSM_GROUPS, SSM_STATE)
    lam_im0 = jnp.pi * jnp.arange(SSM_STATE, dtype=jnp.float32)
    return {
        'x_prompt': nrm((BATCH, SEQ, D), 1.0),
        'x_sample': nrm((DEC_BATCH, DEC_SEQ, D), 1.0),
        'c_prompt': nrm((BATCH, D), 1.0),
        'c_sample': nrm((DEC_BATCH, D), 1.0),
        'state_ssm_re': nrm((N_B_LAYERS, DEC_BATCH, SSM_GROUPS, SSM_STATE), 0.5),
        'state_ssm_im': nrm((N_B_LAYERS, DEC_BATCH, SSM_GROUPS, SSM_STATE), 0.5),
        'ada_w': nrm((DEPTH, D, 6 * D), 0.5 * D ** -0.5),
        'ada_b': nrm((DEPTH, 6 * D), 0.02),
        'norm_mix_g': 1.0 + nrm((DEPTH, D), 0.02),
        'norm_ffn_g': 1.0 + nrm((DEPTH, D), 0.02),
        'final_g': 1.0 + nrm((D,), 0.02),
        'gm_w_in': nrm((N_A_LAYERS, D, 2 * GM_HALF), D ** -0.5),
        'gm_b_in': nrm((N_A_LAYERS, 2 * GM_HALF), 0.02),
        'gm_v_g': 1.0 + nrm((N_A_LAYERS, GM_HALF), 0.02),
        'gm_w_s': nrm((N_A_LAYERS, GM_GROUPS, CHUNK, CHUNK), 0.5 * CHUNK ** -0.5),
        'gm_b_s': 1.0 + nrm((N_A_LAYERS, GM_GROUPS, CHUNK), 0.02),
        'gm_w_out': nrm((N_A_LAYERS, GM_HALF, D), GM_HALF ** -0.5),
        'ssm_w_in': nrm((N_B_LAYERS, D, SSM_WIDTH), D ** -0.5),
        'ssm_lam_re': -0.5 + nrm(ssm_sh, 0.01),
        'ssm_lam_im': lam_im0 + nrm(ssm_sh, 0.01),
        'ssm_log_dt': jax.random.uniform(next(ks), (N_B_LAYERS, SSM_GROUPS), jnp.float32,
                                         math.log(DT_MIN), math.log(DT_MAX)),
        'ssm_b_re': nrm((N_B_LAYERS, SSM_GROUPS, SSM_STATE, SSM_GCH), (2 * SSM_GCH) ** -0.5),
        'ssm_b_im': nrm((N_B_LAYERS, SSM_GROUPS, SSM_STATE, SSM_GCH), (2 * SSM_GCH) ** -0.5),
        'ssm_c_re': nrm((N_B_LAYERS, SSM_GROUPS, SSM_GCH, SSM_STATE), SSM_STATE ** -0.5),
        'ssm_c_im': nrm((N_B_LAYERS, SSM_GROUPS, SSM_GCH, SSM_STATE), SSM_STATE ** -0.5),
        'ssm_d': nrm((N_B_LAYERS, SSM_WIDTH), 0.5),
        'ssm_w_out': nrm((N_B_LAYERS, SSM_WIDTH, 2 * D), SSM_WIDTH ** -0.5),
        'peer_w_q': nrm((DEPTH, D, PEER_HEADS * 2 * PEER_DKEY), D ** -0.5),
        'peer_keys': nrm((DEPTH, PEER_HEADS, 2, PEER_KEYS, PEER_DKEY), PEER_DKEY ** -0.5),
        'peer_u': nrm((DEPTH, PEER_EXPERTS, D), D ** -0.5),
        'peer_v': nrm((DEPTH, PEER_EXPERTS, D), 0.25),
    }


def reference(x_prompt, x_sample, c_prompt, c_sample, state_ssm_re, state_ssm_im,
              ada_w, ada_b, norm_mix_g, norm_ffn_g, final_g,
              gm_w_in, gm_b_in, gm_v_g, gm_w_s, gm_b_s, gm_w_out,
              ssm_w_in, ssm_lam_re, ssm_lam_im, ssm_log_dt, ssm_b_re, ssm_b_im,
              ssm_c_re, ssm_c_im, ssm_d, ssm_w_out,
              peer_w_q, peer_keys, peer_u, peer_v):
    p = dict(ada_w=ada_w, ada_b=ada_b, norm_mix_g=norm_mix_g, norm_ffn_g=norm_ffn_g,
             final_g=final_g, gm_w_in=gm_w_in, gm_b_in=gm_b_in, gm_v_g=gm_v_g,
             gm_w_s=gm_w_s, gm_b_s=gm_b_s, gm_w_out=gm_w_out, ssm_w_in=ssm_w_in,
             ssm_lam_re=ssm_lam_re, ssm_lam_im=ssm_lam_im, ssm_log_dt=ssm_log_dt,
             ssm_b_re=ssm_b_re, ssm_b_im=ssm_b_im, ssm_c_re=ssm_c_re, ssm_c_im=ssm_c_im,
             ssm_d=ssm_d, ssm_w_out=ssm_w_out, peer_w_q=peer_w_q, peer_keys=peer_keys,
             peer_u=peer_u, peer_v=peer_v)
    h0 = jnp.zeros((N_B_LAYERS, x_prompt.shape[0], SSM_GROUPS, SSM_STATE), x_prompt.dtype)
    y_prompt, _, ssm_re_prompt, ssm_im_prompt = trunk(x_prompt, c_prompt, h0, h0, CHUNK, p)
    y_sample, v_sample, ssm_re_sample, ssm_im_sample = trunk(
        x_sample, c_sample, state_ssm_re, state_ssm_im, x_sample.shape[1], p)
    return (y_prompt, y_sample, v_sample, ssm_re_prompt, ssm_im_prompt, ssm_re_sample, ssm_im_sample)
```

```python
import functools

import jax
import jax.numpy as jnp
from jax import lax
from jax.experimental import pallas as pl
from jax.experimental.pallas import tpu as pltpu

EPS = 1e-6
PEER_TOPK = 16
ROW_GROUP = 8
SSM_PACK_GROUPS = 16
VMEM_LIMIT_BYTES = 56 * 1024 * 1024
F32 = jnp.float32
BF16 = jnp.bfloat16


def _params(*sem):
    return pltpu.CompilerParams(dimension_semantics=sem, vmem_limit_bytes=VMEM_LIMIT_BYTES)


def _tile(n, pref, align=128):
    if n <= pref:
        return n
    for cand in range(pref - pref % align, 0, -align):
        if n % cand == 0:
            return cand
    raise ValueError((n, pref, align))


def _ada_kernel(c_ref, w_ref, b_ref, o_ref):
    c = c_ref[...]
    act = (c * (1.0 / (1.0 + jnp.exp(-c)))).astype(BF16)
    o_ref[...] = jnp.dot(act, w_ref[...].astype(BF16), preferred_element_type=F32) + b_ref[...]


def _ada_call(c_all, ada_w, ada_b):
    depth, d, n = ada_w.shape
    m = c_all.shape[0]
    tn = _tile(n, 512)
    return pl.pallas_call(
        _ada_kernel,
        grid=(depth, n // tn),
        in_specs=[pl.BlockSpec((m, d), lambda l, j: (0, 0)),
                  pl.BlockSpec((None, d, tn), lambda l, j: (l, 0, j)),
                  pl.BlockSpec((None, 1, tn), lambda l, j: (l, 0, j))],
        out_specs=pl.BlockSpec((None, m, tn), lambda l, j: (l, 0, j)),
        out_shape=jax.ShapeDtypeStruct((depth, m, n), F32),
        compiler_params=_params("arbitrary", "arbitrary"),
        name="ada_mod",
    )(c_all, ada_w, ada_b.reshape(depth, 1, n))


def _resnorm_kernel(*refs, has_delta, delta_t, emit_h, emit_ht, final):
    it = iter(refs)
    x_ref = next(it)
    if has_delta:
        delta_ref = next(it)
        gate_ref = next(it)
    g_ref = next(it)
    if not final:
        sc_ref = next(it)
        sh_ref = next(it)
    tm, d = x_ref.shape
    r = tm // ROW_GROUP
    x3 = x_ref[...].reshape(r, ROW_GROUP, d)
    if has_delta:
        delta = delta_ref[...]
        if delta_t:
            delta = delta.T
        x3 = x3 + gate_ref[...][:, None, :] * delta.reshape(r, ROW_GROUP, d)
    var = jnp.mean(x3 * x3, axis=-1, keepdims=True)
    nrm = x3 * lax.rsqrt(var + EPS) * g_ref[...]
    if final:
        next(it)[...] = nrm.reshape(tm, d)
        return
    h = (nrm * (1.0 + sc_ref[...][:, None, :]) + sh_ref[...][:, None, :]).reshape(tm, d)
    if has_delta:
        next(it)[...] = x3.reshape(tm, d)
    if emit_h:
        next(it)[...] = h.astype(BF16)
    if emit_ht:
        next(it)[...] = h.T.astype(BF16)


def _resnorm_call(x, gain, me=None, *, delta=None, delta_t=False, me_gate=None, gate_col=None,
                  sc_col=None, sh_col=None, emit_h=True, emit_ht=False, final=False):
    t, d = x.shape
    tm = _tile(t, 256)
    r = tm // ROW_GROUP
    has_delta = delta is not None
    row = pl.BlockSpec((tm, d), lambda i: (i, 0))
    col = pl.BlockSpec((d, tm), lambda i: (0, i))

    def me_spec(k):
        return pl.BlockSpec((r, d), lambda i, k=k: (i, k))

    in_specs, args = [row], [x]
    if has_delta:
        in_specs += [col if delta_t else row, me_spec(gate_col)]
        args += [delta, me_gate]
    in_specs.append(pl.BlockSpec((1, d), lambda i: (0, 0)))
    args.append(gain.reshape(1, d))
    if not final:
        in_specs += [me_spec(sc_col), me_spec(sh_col)]
        args += [me, me]
    out_specs, out_shape = [], []
    if final or has_delta:
        out_specs.append(row)
        out_shape.append(jax.ShapeDtypeStruct((t, d), F32))
    if not final and emit_h:
        out_specs.append(row)
        out_shape.append(jax.ShapeDtypeStruct((t, d), BF16))
    if not final and emit_ht:
        out_specs.append(col)
        out_shape.append(jax.ShapeDtypeStruct((d, t), BF16))
    return pl.pallas_call(
        functools.partial(_resnorm_kernel, has_delta=has_delta, delta_t=delta_t,
                          emit_h=emit_h, emit_ht=emit_ht, final=final),
        grid=(t // tm,),
        in_specs=in_specs, out_specs=out_specs, out_shape=out_shape,
        compiler_params=_params("arbitrary"),
        name="resnorm",
    )(*args)


def _mm_kernel(a_ref, b_ref, *rest, has_bias, act):
    o_ref = rest[-1]
    acc = jnp.dot(a_ref[...], b_ref[...], preferred_element_type=F32)
    if has_bias:
        acc = acc + rest[0][...]
    if act == "gelu":
        acc = jax.nn.gelu(acc)
    o_ref[...] = acc.astype(o_ref.dtype)


def _mm_call(a, b, *, bias=None, act=None, out_dtype=F32, col_start=0, n_out=None, tm=1024, tn=512):
    m, k = a.shape
    n_out = b.shape[1] if n_out is None else n_out
    tm = _tile(m, tm)
    tn = _tile(n_out, tn)
    assert col_start % tn == 0
    off = col_start // tn
    in_specs = [pl.BlockSpec((tm, k), lambda i, j: (i, 0)),
                pl.BlockSpec((k, tn), lambda i, j: (0, j + off))]
    args = [a, b]
    if bias is not None:
        in_specs.append(pl.BlockSpec((1, tn), lambda i, j: (0, j + off)))
        args.append(bias.reshape(1, -1))
    return pl.pallas_call(
        functools.partial(_mm_kernel, has_bias=bias is not None, act=act),
        grid=(m // tm, n_out // tn),
        in_specs=in_specs,
        out_specs=pl.BlockSpec((tm, tn), lambda i, j: (i, j)),
        out_shape=jax.ShapeDtypeStruct((m, n_out), out_dtype),
        compiler_params=_params("arbitrary", "arbitrary"),
        name="matmul",
    )(*args)


def _glu_kernel(a_ref, ba_ref, bg_ref, o_ref):
    a = a_ref[...]
    lin = jnp.dot(a, ba_ref[...], preferred_element_type=F32)
    gat = jnp.dot(a, bg_ref[...], preferred_element_type=F32)
    o_ref[...] = lin * (1.0 / (1.0 + jnp.exp(-gat)))


def _glu_call(a, b, *, tm=1024, tn=256):
    m, k = a.shape
    n = b.shape[1] // 2
    tm = _tile(m, tm)
    tn = _tile(n, tn)
    off = n // tn
    return pl.pallas_call(
        _glu_kernel,
        grid=(m // tm, n // tn),
        in_specs=[pl.BlockSpec((tm, k), lambda i, j: (i, 0)),
                  pl.BlockSpec((k, tn), lambda i, j: (0, j)),
                  pl.BlockSpec((k, tn), lambda i, j: (0, j + off))],
        out_specs=pl.BlockSpec((tm, tn), lambda i, j: (i, j)),
        out_shape=jax.ShapeDtypeStruct((m, n), F32),
        compiler_params=_params("arbitrary", "arbitrary"),
        name="glu_matmul",
    )(a, b, b)


def _gmlp_gate_kernel(u_ref, v_ref, vg_ref, w_ref, b_ref, o_ref, vs_ref, *, groups, n_prompt_tiles):
    v = v_ref[...]
    var = jnp.mean(v * v, axis=-1, keepdims=True)
    vn = v * lax.rsqrt(var + EPS) * vg_ref[...]

    @pl.when(pl.program_id(0) >= n_prompt_tiles)
    def _():
        vs_ref[...] = vn

    gdim = v.shape[1] // groups
    for g in range(groups):
        sl = slice(g * gdim, (g + 1) * gdim)
        s = jnp.dot(w_ref[g], vn[:, sl].astype(BF16), preferred_element_type=F32) + b_ref[g]
        o_ref[:, sl] = (u_ref[:, sl].astype(F32) * s).astype(BF16)


def _gmlp_gate_call(u, v_raw, v_g, wmix, bmix, n_prompt_tiles):
    t, hw = u.shape
    _, groups, blk, _ = wmix.shape
    n_tiles = t // blk
    kind = lambda i: jnp.where(i >= n_prompt_tiles, 1, 0)
    return pl.pallas_call(
        functools.partial(_gmlp_gate_kernel, groups=groups, n_prompt_tiles=n_prompt_tiles),
        grid=(n_tiles,),
        in_specs=[pl.BlockSpec((blk, hw), lambda i: (i, 0)),
                  pl.BlockSpec((blk, hw), lambda i: (i, 0)),
                  pl.BlockSpec((1, hw), lambda i: (0, 0)),
                  pl.BlockSpec((None, groups, blk, blk), lambda i: (kind(i), 0, 0, 0)),
                  pl.BlockSpec((None, groups, blk, 1), lambda i: (kind(i), 0, 0, 0))],
        out_specs=[pl.BlockSpec((blk, hw), lambda i: (i, 0)),
                   pl.BlockSpec((blk, hw), lambda i: (jnp.maximum(i - n_prompt_tiles, 0), 0))],
        out_shape=[jax.ShapeDtypeStruct((t, hw), BF16),
                   jax.ShapeDtypeStruct(((n_tiles - n_prompt_tiles) * blk, hw), F32)],
        compiler_params=_params("arbitrary"),
        name="gmlp_gate",
    )(u, v_raw, v_g.reshape(1, hw), wmix, bmix)


def _ssm_disc_kernel(lre_ref, lim_ref, ldt_ref, are_ref, aim_ref, fre_ref, fim_ref):
    lre = lre_ref[...]
    lim = lim_ref[...]
    dt = jnp.exp(ldt_ref[...])
    mag = jnp.exp(lre * dt)
    ang = lim * dt
    a_re = mag * jnp.cos(ang)
    a_im = mag * jnp.sin(ang)
    n_re = a_re - 1
    den = lre * lre + lim * lim
    are_ref[...] = a_re
    aim_ref[...] = a_im
    fre_ref[...] = (n_re * lre + a_im * lim) / den
    fim_ref[...] = (a_im * lre - n_re * lim) / den


def _ssm_disc_call(lam_re, lam_im, log_dt):
    g, p = lam_re.shape
    shp = jax.ShapeDtypeStruct((g, p), F32)
    return pl.pallas_call(_ssm_disc_kernel, out_shape=[shp] * 4, name="ssm_discretize")(
        lam_re, lam_im, log_dt.reshape(g, 1))


def _ssm_core_kernel(xs_ref, bd_ref, cd_ref, a_ref, d_ref, h0_ref, y_ref, hf_ref, bu_scr, st_scr,
                     *, col_chunk):
    tl, nb, ch = xs_ref.shape
    ns = a_ref.shape[1] // 2

    @pl.when(pl.program_id(1) == 0)
    def _():
        st_scr[...] = h0_ref[...]

    xs = xs_ref[...].reshape(tl * nb, ch)
    bu = jnp.dot(xs.astype(BF16), bd_ref[...], preferred_element_type=F32)
    bu_scr[...] = bu.reshape(tl, nb, 2 * ns)

    def row_block(rb, carry):
        rows = pl.ds(pl.multiple_of(rb * ROW_GROUP, ROW_GROUP), ROW_GROUP)
        for cc in range(ns // col_chunk):
            cre = slice(cc * col_chunk, (cc + 1) * col_chunk)
            cim = slice(ns + cc * col_chunk, ns + (cc + 1) * col_chunk)
            a_re = jnp.broadcast_to(a_ref[:, cre], (ROW_GROUP, col_chunk))
            a_im = jnp.broadcast_to(a_ref[:, cim], (ROW_GROUP, col_chunk))

            def step(l, st):
                s_re, s_im = st
                n_re = a_re * s_re - a_im * s_im + bu_scr[l, rows, cre]
                n_im = a_re * s_im + a_im * s_re + bu_scr[l, rows, cim]
                bu_scr[l, rows, cre] = n_re
                bu_scr[l, rows, cim] = n_im
                return n_re, n_im

            s_re, s_im = lax.fori_loop(0, tl, step, (st_scr[rows, cre], st_scr[rows, cim]),
                                       unroll=min(tl, 8))
            st_scr[rows, cre] = s_re
            st_scr[rows, cim] = s_im
        return carry

    lax.fori_loop(0, nb // ROW_GROUP, row_block, 0)
    st = bu_scr[...].reshape(tl * nb, 2 * ns).astype(BF16)
    y = jnp.dot(st, cd_ref[...], preferred_element_type=F32) + d_ref[...] * xs
    y_ref[...] = jax.nn.gelu(y).reshape(tl, nb, ch).astype(BF16)
    hf_ref[...] = st_scr[...]


def _ssm_core_call(xs_lb, bd, cd, a_cat, d_skip, h0, tl):
    L, nb, w = xs_lb.shape
    packs, ch, ns2 = bd.shape
    return pl.pallas_call(
        functools.partial(_ssm_core_kernel, col_chunk=min(ns2 // 2, 512)),
        grid=(packs, L // tl),
        in_specs=[pl.BlockSpec((tl, nb, ch), lambda c, t: (t, 0, c)),
                  pl.BlockSpec((None, ch, ns2), lambda c, t: (c, 0, 0)),
                  pl.BlockSpec((None, ns2, ch), lambda c, t: (c, 0, 0)),
                  pl.BlockSpec((None, 1, ns2), lambda c, t: (c, 0, 0)),
                  pl.BlockSpec((1, ch), lambda c, t: (0, c)),
                  pl.BlockSpec((None, nb, ns2), lambda c, t: (c, 0, 0))],
        out_specs=[pl.BlockSpec((tl, nb, ch), lambda c, t: (t, 0, c)),
                   pl.BlockSpec((None, nb, ns2), lambda c, t: (c, 0, 0))],
        out_shape=[jax.ShapeDtypeStruct((L, nb, w), BF16),
                   jax.ShapeDtypeStruct((packs, nb, ns2), F32)],
        scratch_shapes=[pltpu.VMEM((tl, nb, ns2), F32), pltpu.VMEM((nb, ns2), F32)],
        compiler_params=_params("arbitrary", "arbitrary"),
        name="ssm_core",
    )(xs_lb, bd, cd, a_cat, d_skip.reshape(1, w), h0)


def _ssm_operands(f_re, f_im, a_re, a_im, b_re, b_im, c_re, c_im):
    g, p, k = b_re.shape
    pg = SSM_PACK_GROUPS
    packs = g // pg
    bb_re = f_re[..., None] * b_re - f_im[..., None] * b_im
    bb_im = f_re[..., None] * b_im + f_im[..., None] * b_re
    eye = jnp.eye(pg, dtype=F32)

    def bdiag(m):
        m = m.reshape(packs, pg, p, k).transpose(0, 1, 3, 2)
        return (m[:, :, :, None, :] * eye[None, :, None, :, None]).reshape(packs, pg * k, pg * p)

    def cdiag(m):
        m = m.reshape(packs, pg, k, p).transpose(0, 1, 3, 2)
        return (m[:, :, :, None, :] * eye[None, :, None, :, None]).reshape(packs, pg * p, pg * k)

    bd = jnp.concatenate([bdiag(bb_re), bdiag(bb_im)], axis=2).astype(BF16)
    cd = jnp.concatenate([cdiag(c_re), cdiag(-c_im)], axis=1).astype(BF16)
    a_cat = jnp.concatenate([a_re.reshape(packs, 1, pg * p), a_im.reshape(packs, 1, pg * p)], axis=2)
    return bd, cd, a_cat


def _pack_state(s_re, s_im, nb):
    b, g, p = s_re.shape
    packs = g // SSM_PACK_GROUPS

    def one(s):
        s = s.reshape(b, packs, SSM_PACK_GROUPS * p).transpose(1, 0, 2)
        return jnp.pad(s, ((0, 0), (0, nb - b), (0, 0)))

    return jnp.concatenate([one(s_re), one(s_im)], axis=2)


def _unpack_state(hf, b, g, p):
    ns = hf.shape[2] // 2

    def one(s):
        return s[:, :b].transpose(1, 0, 2).reshape(b, g, p)

    return one(hf[:, :, :ns]), one(hf[:, :, ns:])


def _peer_select_kernel(ht_ref, wq_ref, keys_ref, a0_ref, e0_ref, a1_ref, e1_ref, thr_ref,
                        top_scr, cand_scr):
    dk = keys_ref.shape[3]
    q = jnp.dot(wq_ref[...], ht_ref[...], preferred_element_type=F32)
    scores = []
    for z in range(2):
        s = jnp.dot(keys_ref[0, z], q[z * dk:(z + 1) * dk], preferred_element_type=F32,
                    precision=lax.Precision.HIGHEST)
        scores.append(s)
        cur = s
        for k in range(PEER_TOPK):
            m = jnp.max(cur, axis=0, keepdims=True)
            top_scr[z, k:k + 1, :] = m
            cur = jnp.where(cur == m, -jnp.inf, cur)
    top1 = top_scr[1]
    for i in range(PEER_TOPK):
        cand_scr[i * PEER_TOPK:(i + 1) * PEER_TOPK, :] = top_scr[0, i:i + 1, :] + top1
    cur = cand_scr[...]
    best = top_scr[0, 0:1, :] + top_scr[1, 0:1, :]
    zsum = jnp.zeros_like(best)
    m = best
    for k in range(PEER_TOPK):
        m = jnp.max(cur, axis=0, keepdims=True)
        zsum = zsum + jnp.exp(m - best)
        cur = jnp.where(cur == m, -jnp.inf, cur)
    thr_ref[0] = m
    a0_ref[0] = scores[0]
    a1_ref[0] = scores[1]
    e0_ref[0] = jnp.exp(scores[0] - top_scr[0, 0:1, :]) / zsum
    e1_ref[0] = jnp.exp(scores[1] - top_scr[1, 0:1, :])


def _peer_select_call(ht, wq_t, keys):
    d, t = ht.shape
    heads, _, n_keys, dk = keys.shape
    tt = _tile(t, 512)
    sc_spec = pl.BlockSpec((1, n_keys, tt), lambda i, h: (h, 0, i))
    sc_shape = jax.ShapeDtypeStruct((heads, n_keys, t), F32)
    return pl.pallas_call(
        _peer_select_kernel,
        grid=(t // tt, heads),
        in_specs=[pl.BlockSpec((d, tt), lambda i, h: (0, i)),
                  pl.BlockSpec((2 * dk, d), lambda i, h: (h, 0)),
                  pl.BlockSpec((1, 2, n_keys, dk), lambda i, h: (h, 0, 0, 0))],
        out_specs=[sc_spec, sc_spec, sc_spec, sc_spec,
                   pl.BlockSpec((1, 1, tt), lambda i, h: (h, 0, i))],
        out_shape=[sc_shape, sc_shape, sc_shape, sc_shape,
                   jax.ShapeDtypeStruct((heads, 1, t), F32)],
        scratch_shapes=[pltpu.VMEM((2, PEER_TOPK, tt), F32),
                        pltpu.VMEM((PEER_TOPK * PEER_TOPK, tt), F32)],
        compiler_params=_params("arbitrary", "arbitrary"),
        name="peer_select",
    )(ht, wq_t, keys)


def _peer_dense_kernel(ht_ref, u_ref, vt_ref, a0t_ref, e0t_ref, a1_ref, e1_ref, thr_ref, o_ref, p_scr):
    nsub, heads, tt = a0t_ref.shape
    n2 = a1_ref.shape[1]

    @pl.when(pl.program_id(1) == 0)
    def _():
        o_ref[...] = jnp.zeros_like(o_ref)

    act = jax.nn.gelu(jnp.dot(u_ref[...], ht_ref[...], preferred_element_type=F32))
    for r in range(nsub):
        w = jnp.zeros((n2, tt), F32)
        for h in range(heads):
            hit = (a0t_ref[r, h:h + 1, :] + a1_ref[h]) >= thr_ref[h]
            w = w + jnp.where(hit, e0t_ref[r, h:h + 1, :] * e1_ref[h], 0.0)
        p_scr[r * n2:(r + 1) * n2, :] = (w * act[r * n2:(r + 1) * n2, :]).astype(BF16)
    o_ref[...] += jnp.dot(vt_ref[...], p_scr[...], preferred_element_type=F32)


def _peer_dense_call(ht, u_bf, vt_bf, a0t, e0t, a1, e1, thr, *, ec=256):
    d, t = ht.shape
    n_exp = u_bf.shape[0]
    heads, n2, _ = a1.shape
    tt = _tile(t, 512)
    ec = _tile(n_exp, ec)
    nsub = ec // n2
    row_spec = pl.BlockSpec((nsub, heads, tt), lambda i, j: (j, 0, i))
    full_spec = pl.BlockSpec((heads, n2, tt), lambda i, j: (0, 0, i))
    return pl.pallas_call(
        _peer_dense_kernel,
        grid=(t // tt, n_exp // ec),
        in_specs=[pl.BlockSpec((d, tt), lambda i, j: (0, i)),
                  pl.BlockSpec((ec, d), lambda i, j: (j, 0)),
                  pl.BlockSpec((d, ec), lambda i, j: (0, j)),
                  row_spec, row_spec, full_spec, full_spec,
                  pl.BlockSpec((heads, 1, tt), lambda i, j: (0, 0, i))],
        out_specs=pl.BlockSpec((d, tt), lambda i, j: (0, i)),
        out_shape=jax.ShapeDtypeStruct((d, t), F32),
        scratch_shapes=[pltpu.VMEM((ec, tt), BF16)],
        compiler_params=_params("arbitrary", "arbitrary"),
        name="peer_dense",
    )(ht, u_bf, vt_bf, a0t, e0t, a1, e1, thr)


def _peer_ffn_t(ht, w_q, keys, u_tab, v_tab):
    a0, e0, a1, e1, thr = _peer_select_call(ht, w_q.T.astype(BF16), keys)
    return _peer_dense_call(ht, u_tab.astype(BF16), v_tab.T.astype(BF16),
                            a0.transpose(1, 0, 2), e0.transpose(1, 0, 2), a1, e1, thr)


def kernel(x_prompt, x_sample, c_prompt, c_sample, state_ssm_re, state_ssm_im, ada_w, ada_b, norm_mix_g, norm_ffn_g, final_g, gm_w_in, gm_b_in, gm_v_g, gm_w_s, gm_b_s, gm_w_out, ssm_w_in, ssm_lam_re, ssm_lam_im, ssm_log_dt, ssm_b_re, ssm_b_im, ssm_c_re, ssm_c_im, ssm_d, ssm_w_out, peer_w_q, peer_keys, peer_u, peer_v):
    bp, seq, d = x_prompt.shape
    bs, dseq, _ = x_sample.shape
    depth = ada_w.shape[0]
    tp, ts = bp * seq, bs * dseq
    chunk = gm_w_s.shape[2]
    hw = gm_w_in.shape[2] // 2
    n_groups, n_state = ssm_lam_re.shape[1:]
    w_ssm = ssm_w_in.shape[2]
    assert dseq == ROW_GROUP and seq % chunk == 0 and ts % chunk == 0 and bp <= ROW_GROUP

    x = jnp.concatenate([x_prompt.reshape(tp, d), x_sample.reshape(ts, d)], axis=0)

    m_real = bp + bs
    m_pad = -(-m_real // ROW_GROUP) * ROW_GROUP
    c_all = jnp.pad(jnp.concatenate([c_prompt, c_sample], axis=0), ((0, m_pad - m_real), (0, 0)))
    mods = _ada_call(c_all, ada_w, ada_b)

    def group_rows(i):
        return jnp.concatenate([jnp.repeat(mods[i, :bp], seq // ROW_GROUP, axis=0),
                                mods[i, bp:m_real]], axis=0)

    tril = jnp.tril(jnp.ones((chunk, chunk), F32))
    rep = chunk // dseq
    tril_s = jnp.tril(jnp.ones((dseq, dseq), F32))

    def spatial_operands(ia):
        w_p = gm_w_s[ia] * tril
        w_s = jnp.einsum("ab,gts->gatbs", jnp.eye(rep, dtype=F32),
                         gm_w_s[ia][:, :dseq, :dseq] * tril_s).reshape(-1, chunk, chunk)
        b_p = gm_b_s[ia][:, :, None]
        b_s = jnp.tile(gm_b_s[ia][:, :dseq], (1, rep))[:, :, None]
        return jnp.stack([w_p, w_s]).astype(BF16), jnp.stack([b_p, b_s])

    new_v, st_p, st_s = [], [], []
    peer_t = None
    me_prev = None
    ia = ib = 0
    for i in range(depth):
        me = group_rows(i)
        if i == 0:
            (h,) = _resnorm_call(x, norm_mix_g[i], me, sc_col=1, sh_col=0)
        else:
            x, h = _resnorm_call(x, norm_mix_g[i], me, delta=peer_t, delta_t=True, me_gate=me_prev,
                                 gate_col=5, sc_col=1, sh_col=0)
        if i % 2 == 0:
            w_in = gm_w_in[ia].astype(BF16)
            u = _mm_call(h, w_in, bias=gm_b_in[ia], act="gelu", out_dtype=BF16, col_start=0, n_out=hw)
            v_raw = _mm_call(h, w_in, bias=gm_b_in[ia], act="gelu", out_dtype=F32, col_start=hw, n_out=hw)
            wmix, bmix = spatial_operands(ia)
            gated, v_s = _gmlp_gate_call(u, v_raw, gm_v_g[ia], wmix, bmix, tp // chunk)
            new_v.append(v_s.reshape(bs, dseq, hw))
            mix = _mm_call(gated, gm_w_out[ia].astype(BF16), tm=512, tn=512)
            ia += 1
        else:
            xs = _mm_call(h, ssm_w_in[ib].astype(BF16))
            a_re, a_im, f_re, f_im = _ssm_disc_call(ssm_lam_re[ib], ssm_lam_im[ib], ssm_log_dt[ib])
            bd, cd, a_cat = _ssm_operands(f_re, f_im, a_re, a_im, ssm_b_re[ib], ssm_b_im[ib],
                                          ssm_c_re[ib], ssm_c_im[ib])
            xs_p = jnp.pad(xs[:tp].reshape(bp, seq, w_ssm).transpose(1, 0, 2),
                           ((0, 0), (0, ROW_GROUP - bp), (0, 0)))
            xs_s = xs[tp:].reshape(bs, dseq, w_ssm).transpose(1, 0, 2)
            zero = jnp.zeros((bp, n_groups, n_state), F32)
            y_p, hf_p = _ssm_core_call(xs_p, bd, cd, a_cat, ssm_d[ib], _pack_state(zero, zero, ROW_GROUP),
                                       tl=_tile(seq, 128))
            y_s, hf_s = _ssm_core_call(xs_s, bd, cd, a_cat, ssm_d[ib],
                                       _pack_state(state_ssm_re[ib], state_ssm_im[ib], bs), tl=dseq)
            st_p.append(_unpack_state(hf_p, bp, n_groups, n_state))
            st_s.append(_unpack_state(hf_s, bs, n_groups, n_state))
            yg = jnp.concatenate([y_p[:, :bp].transpose(1, 0, 2).reshape(tp, w_ssm),
                                  y_s.transpose(1, 0, 2).reshape(ts, w_ssm)], axis=0)
            mix = _glu_call(yg, ssm_w_out[ib].astype(BF16))
            ib += 1
        x, ht = _resnorm_call(x, norm_ffn_g[i], me, delta=mix, me_gate=me, gate_col=2, sc_col=4,
                              sh_col=3, emit_h=False, emit_ht=True)
        peer_t = _peer_ffn_t(ht, peer_w_q[i], peer_keys[i], peer_u[i], peer_v[i])
        me_prev = me
    (y,) = _resnorm_call(x, final_g, delta=peer_t, delta_t=True, me_gate=me_prev, gate_col=5, final=True)

    y_prompt = y[:tp].reshape(bp, seq, d)
    y_sample = y[tp:].reshape(bs, dseq, d)
    return (y_prompt, y_sample, jnp.stack(new_v),
            jnp.stack([s[0] for s in st_p]), jnp.stack([s[1] for s in st_p]),
            jnp.stack([s[0] for s in st_s]), jnp.stack([s[1] for s in st_s]))
```

```python
import functools
import math

import jax
import jax.numpy as jnp
from jax import lax
from jax.experimental import pallas as pl
from jax.experimental.pallas import tpu as pltpu

EPS = 1e-6
PEER_TOPK = 16
ROW_GROUP = 8
SSM_PACK_GROUPS = 16
VMEM_LIMIT_BYTES = 56 * 1024 * 1024
F32 = jnp.float32
BF16 = jnp.bfloat16


def _params(*sem):
    return pltpu.CompilerParams(dimension_semantics=sem, vmem_limit_bytes=VMEM_LIMIT_BYTES)


def _tile(n, pref, align=128):
    if n <= pref:
        return n
    for cand in range(pref - pref % align, 0, -align):
        if n % cand == 0:
            return cand
    raise ValueError((n, pref, align))


def _ada_kernel(c_ref, w_ref, b_ref, o_ref):
    c = c_ref[...]
    act = (c * (1.0 / (1.0 + jnp.exp(-c)))).astype(BF16)
    o_ref[...] = jnp.dot(act, w_ref[...].astype(BF16), preferred_element_type=F32) + b_ref[...]


def _ada_call(c_all, ada_w, ada_b):
    depth, d, n = ada_w.shape
    m = c_all.shape[0]
    tn = _tile(n, 512)
    return pl.pallas_call(
        _ada_kernel,
        grid=(depth, n // tn),
        in_specs=[pl.BlockSpec((m, d), lambda l, j: (0, 0)),
                  pl.BlockSpec((None, d, tn), lambda l, j: (l, 0, j)),
                  pl.BlockSpec((None, 1, tn), lambda l, j: (l, 0, j))],
        out_specs=pl.BlockSpec((None, m, tn), lambda l, j: (l, 0, j)),
        out_shape=jax.ShapeDtypeStruct((depth, m, n), F32),
        compiler_params=_params("arbitrary", "arbitrary"),
        name="ada_mod",
    )(c_all, ada_w, ada_b.reshape(depth, 1, n))


def _resnorm_kernel(*refs, has_delta, delta_t, emit_h, emit_ht, final):
    it = iter(refs)
    x_ref = next(it)
    if has_delta:
        delta_ref = next(it)
        gate_ref = next(it)
    g_ref = next(it)
    if not final:
        sc_ref = next(it)
        sh_ref = next(it)
    tm, d = x_ref.shape
    r = tm // ROW_GROUP
    x3 = x_ref[...].reshape(r, ROW_GROUP, d)
    if has_delta:
        delta = delta_ref[...]
        if delta_t:
            delta = delta.T
        x3 = x3 + gate_ref[...][:, None, :] * delta.reshape(r, ROW_GROUP, d)
    var = jnp.mean(x3 * x3, axis=-1, keepdims=True)
    nrm = x3 * lax.rsqrt(var + EPS) * g_ref[...]
    if final:
        next(it)[...] = nrm.reshape(tm, d)
        return
    h = (nrm * (1.0 + sc_ref[...][:, None, :]) + sh_ref[...][:, None, :]).reshape(tm, d)
    if has_delta:
        next(it)[...] = x3.reshape(tm, d)
    if emit_h:
        next(it)[...] = h.astype(BF16)
    if emit_ht:
        next(it)[...] = h.T.astype(BF16)


def _resnorm_tile(groups):
    bp, seq, bs = groups
    return _tile(math.gcd(seq, bs * ROW_GROUP), 256)


def _group_rows(mod, groups):
    bp, _, bs = groups
    r = _resnorm_tile(groups) // ROW_GROUP
    return jnp.concatenate([jnp.repeat(mod[:bp], r, axis=0), mod[bp:bp + bs]], axis=0)


def _resnorm_call(x, gain, groups, me=None, *, delta=None, delta_t=False, me_gate=None, gate_col=None,
                  sc_col=None, sh_col=None, emit_h=True, emit_ht=False, final=False):
    t, d = x.shape
    bp, seq, _ = groups
    tm = _resnorm_tile(groups)
    r = tm // ROW_GROUP
    tps = seq // tm
    n_prompt_tiles = bp * tps
    has_delta = delta is not None
    row = pl.BlockSpec((tm, d), lambda i: (i, 0))
    col = pl.BlockSpec((d, tm), lambda i: (0, i))

    def me_spec(k):
        return pl.BlockSpec(
            (r, d), lambda i, k=k: (jnp.where(i < n_prompt_tiles, i // tps, bp + i - n_prompt_tiles), k))

    in_specs, args = [row], [x]
    if has_delta:
        in_specs += [col if delta_t else row, me_spec(gate_col)]
        args += [delta, me_gate]
    in_specs.append(pl.BlockSpec((1, d), lambda i: (0, 0)))
    args.append(gain.reshape(1, d))
    if not final:
        in_specs += [me_spec(sc_col), me_spec(sh_col)]
        args += [me, me]
    out_specs, out_shape = [], []
    if final or has_delta:
        out_specs.append(row)
        out_shape.append(jax.ShapeDtypeStruct((t, d), F32))
    if not final and emit_h:
        out_specs.append(row)
        out_shape.append(jax.ShapeDtypeStruct((t, d), BF16))
    if not final and emit_ht:
        out_specs.append(col)
        out_shape.append(jax.ShapeDtypeStruct((d, t), BF16))
    return pl.pallas_call(
        functools.partial(_resnorm_kernel, has_delta=has_delta, delta_t=delta_t,
                          emit_h=emit_h, emit_ht=emit_ht, final=final),
        grid=(t // tm,),
        in_specs=in_specs, out_specs=out_specs, out_shape=out_shape,
        compiler_params=_params("arbitrary"),
        name="resnorm",
    )(*args)


def _mm_kernel(a_ref, b_ref, *rest, has_bias, act):
    o_ref = rest[-1]
    acc = jnp.dot(a_ref[...], b_ref[...], preferred_element_type=F32)
    if has_bias:
        acc = acc + rest[0][...]
    if act == "gelu":
        acc = jax.nn.gelu(acc)
    o_ref[...] = acc.astype(o_ref.dtype)


def _mm_call(a, b, layer, *, bias=None, act=None, out_dtype=F32, col_start=0, n_out=None,
             row_start=0, m_rows=None, seq_major=None, tm=1024, tn=512):
    k = a.shape[1]
    m = a.shape[0] if m_rows is None else m_rows
    n_out = b.shape[2] if n_out is None else n_out
    tm = _tile(m if seq_major is None else seq_major[1], tm)
    tn = _tile(n_out, tn)
    assert col_start % tn == 0 and row_start % tm == 0
    off, roff, nj = col_start // tn, row_start // tm, n_out // tn
    in_specs = [pl.BlockSpec((tm, k), lambda i, j: (i + roff, 0)),
                pl.BlockSpec((None, k, tn), lambda i, j: (layer, 0, j + off))]
    args = [a, b]
    if bias is not None:
        in_specs.append(pl.BlockSpec((None, 1, tn), lambda i, j: (layer, 0, j + off)))
        args.append(bias.reshape(bias.shape[0], 1, -1))
    if seq_major is None:
        out_spec = pl.BlockSpec((tm, tn), lambda i, j: (i, j))
        out_shape = jax.ShapeDtypeStruct((m, n_out), out_dtype)
    else:
        n_seq, seq_len = seq_major
        tps = seq_len // tm
        out_spec = pl.BlockSpec((tm, tn), lambda i, j: (i % tps, (i // tps) * nj + j))
        out_shape = jax.ShapeDtypeStruct((seq_len, n_seq * n_out), out_dtype)
    return pl.pallas_call(
        functools.partial(_mm_kernel, has_bias=bias is not None, act=act),
        grid=(m // tm, nj),
        in_specs=in_specs,
        out_specs=out_spec,
        out_shape=out_shape,
        compiler_params=_params("arbitrary", "arbitrary"),
        name="matmul",
    )(*args)


def _glu_kernel(a_ref, ba_ref, bg_ref, o_ref):
    a = a_ref[...]
    lin = jnp.dot(a, ba_ref[...], preferred_element_type=F32)
    gat = jnp.dot(a, bg_ref[...], preferred_element_type=F32)
    o_ref[...] = lin * (1.0 / (1.0 + jnp.exp(-gat)))


def _glu_call(a, b, layer, *, seq_major=None, tm=1024, tn=256):
    k = b.shape[1]
    n = b.shape[2] // 2
    if seq_major is None:
        m = a.shape[0]
        tm = _tile(m, tm)
        a_spec = pl.BlockSpec((tm, k), lambda i, j: (i, 0))
    else:
        n_seq, seq_len = seq_major
        m = n_seq * seq_len
        tm = _tile(seq_len, tm)
        tps = seq_len // tm
        a_spec = pl.BlockSpec((tm, k), lambda i, j: (i % tps, i // tps))
    tn = _tile(n, tn)
    off = n // tn
    return pl.pallas_call(
        _glu_kernel,
        grid=(m // tm, n // tn),
        in_specs=[a_spec,
                  pl.BlockSpec((None, k, tn), lambda i, j: (layer, 0, j)),
                  pl.BlockSpec((None, k, tn), lambda i, j: (layer, 0, j + off))],
        out_specs=pl.BlockSpec((tm, tn), lambda i, j: (i, j)),
        out_shape=jax.ShapeDtypeStruct((m, n), F32),
        compiler_params=_params("arbitrary", "arbitrary"),
        name="glu_matmul",
    )(a, b, b)


def _gmlp_gate_kernel(u_ref, v_ref, vg_ref, w_ref, b_ref, o_ref, vs_ref, *, groups, n_prompt_tiles):
    v = v_ref[...]
    var = jnp.mean(v * v, axis=-1, keepdims=True)
    vn = v * lax.rsqrt(var + EPS) * vg_ref[...]

    @pl.when(pl.program_id(0) >= n_prompt_tiles)
    def _():
        vs_ref[...] = vn

    gdim = v.shape[1] // groups
    for g in range(groups):
        sl = slice(g * gdim, (g + 1) * gdim)
        s = jnp.dot(w_ref[g], vn[:, sl].astype(BF16), preferred_element_type=F32) + b_ref[g]
        o_ref[:, sl] = (u_ref[:, sl].astype(F32) * s).astype(BF16)


def _gmlp_gate_call(u, v_raw, v_g, wmix, bmix, n_prompt_tiles):
    t, hw = u.shape
    _, groups, blk, _ = wmix.shape
    n_tiles = t // blk
    kind = lambda i: jnp.where(i >= n_prompt_tiles, 1, 0)
    return pl.pallas_call(
        functools.partial(_gmlp_gate_kernel, groups=groups, n_prompt_tiles=n_prompt_tiles),
        grid=(n_tiles,),
        in_specs=[pl.BlockSpec((blk, hw), lambda i: (i, 0)),
                  pl.BlockSpec((blk, hw), lambda i: (i, 0)),
                  pl.BlockSpec((1, hw), lambda i: (0, 0)),
                  pl.BlockSpec((None, groups, blk, blk), lambda i: (kind(i), 0, 0, 0)),
                  pl.BlockSpec((None, groups, blk, 1), lambda i: (kind(i), 0, 0, 0))],
        out_specs=[pl.BlockSpec((blk, hw), lambda i: (i, 0)),
                   pl.BlockSpec((blk, hw), lambda i: (jnp.maximum(i - n_prompt_tiles, 0), 0))],
        out_shape=[jax.ShapeDtypeStruct((t, hw), BF16),
                   jax.ShapeDtypeStruct(((n_tiles - n_prompt_tiles) * blk, hw), F32)],
        compiler_params=_params("arbitrary"),
        name="gmlp_gate",
    )(u, v_raw, v_g.reshape(1, hw), wmix, bmix)


def _ssm_disc_kernel(lre_ref, lim_ref, ldt_ref, are_ref, aim_ref, fre_ref, fim_ref):
    lre = lre_ref[...]
    lim = lim_ref[...]
    dt = jnp.exp(ldt_ref[...])
    mag = jnp.exp(lre * dt)
    ang = lim * dt
    a_re = mag * jnp.cos(ang)
    a_im = mag * jnp.sin(ang)
    n_re = a_re - 1
    den = lre * lre + lim * lim
    are_ref[...] = a_re
    aim_ref[...] = a_im
    fre_ref[...] = (n_re * lre + a_im * lim) / den
    fim_ref[...] = (a_im * lre - n_re * lim) / den


def _ssm_disc_call(lam_re, lam_im, log_dt):
    g, p = lam_re.shape
    shp = jax.ShapeDtypeStruct((g, p), F32)
    return pl.pallas_call(_ssm_disc_kernel, out_shape=[shp] * 4, name="ssm_discretize")(
        lam_re, lam_im, log_dt.reshape(g, 1))


def _ssm_core_kernel(xs_ref, bd_ref, cd_ref, a_ref, d_ref, h0_ref, y_ref, hf_ref, bu_scr, st_scr,
                     *, nb, col_chunk):
    n_rows, ch = xs_ref.shape
    ns = a_ref.shape[1] // 2
    half = ROW_GROUP // 2

    @pl.when(pl.program_id(1) == 0)
    def _():
        st_scr[...] = h0_ref[...]

    xs = xs_ref[...]
    bu_scr[...] = jnp.dot(xs.astype(BF16), bd_ref[...], preferred_element_type=F32)

    def columns(cc):
        cre = slice(cc * col_chunk, (cc + 1) * col_chunk)
        cim = slice(ns + cc * col_chunk, ns + (cc + 1) * col_chunk)
        a_re = jnp.broadcast_to(a_ref[:, cre], (ROW_GROUP, col_chunk))
        a_im = jnp.broadcast_to(a_ref[:, cim], (ROW_GROUP, col_chunk))
        return cre, cim, a_re, a_im

    if nb % ROW_GROUP == 0:
        def row_block(rb, carry):
            srows = pl.ds(pl.multiple_of(rb * ROW_GROUP, ROW_GROUP), ROW_GROUP)
            for cc in range(ns // col_chunk):
                cre, cim, a_re, a_im = columns(cc)

                def step(l, st):
                    s_re, s_im = st
                    rows = pl.ds(pl.multiple_of(l * nb + rb * ROW_GROUP, ROW_GROUP), ROW_GROUP)
                    n_re = a_re * s_re - a_im * s_im + bu_scr[rows, cre]
                    n_im = a_re * s_im + a_im * s_re + bu_scr[rows, cim]
                    bu_scr[rows, cre] = n_re
                    bu_scr[rows, cim] = n_im
                    return n_re, n_im

                steps = n_rows // nb
                s_re, s_im = lax.fori_loop(0, steps, step, (st_scr[srows, cre], st_scr[srows, cim]),
                                           unroll=min(steps, 8))
                st_scr[srows, cre] = s_re
                st_scr[srows, cim] = s_im
            return carry

        lax.fori_loop(0, nb // ROW_GROUP, row_block, 0)
    else:
        assert nb == half
        hi = lax.broadcasted_iota(jnp.int32, (ROW_GROUP, col_chunk), 0) >= half
        for cc in range(ns // col_chunk):
            cre, cim, a_re, a_im = columns(cc)
            ahi_re = jnp.where(hi, a_re, 0.0)
            ahi_im = jnp.where(hi, a_im, 0.0)
            ap_re = jnp.where(hi, a_re * a_re - a_im * a_im, a_re)
            ap_im = jnp.where(hi, 2.0 * a_re * a_im, a_im)

            def step(s, st):
                c_re, c_im = st
                rows = pl.ds(pl.multiple_of(s * ROW_GROUP, ROW_GROUP), ROW_GROUP)
                v_re = bu_scr[rows, cre]
                v_im = bu_scr[rows, cim]
                r_re = pltpu.roll(v_re, half, 0)
                r_im = pltpu.roll(v_im, half, 0)
                o_re = ap_re * c_re - ap_im * c_im + (v_re + (ahi_re * r_re - ahi_im * r_im))
                o_im = ap_re * c_im + ap_im * c_re + (v_im + (ahi_re * r_im + ahi_im * r_re))
                bu_scr[rows, cre] = o_re
                bu_scr[rows, cim] = o_im
                return (jnp.where(hi, o_re, pltpu.roll(o_re, half, 0)),
                        jnp.where(hi, o_im, pltpu.roll(o_im, half, 0)))

            c_re, c_im = lax.fori_loop(0, n_rows // ROW_GROUP, step, (st_scr[:, cre], st_scr[:, cim]),
                                       unroll=4)
            st_scr[:, cre] = c_re
            st_scr[:, cim] = c_im

    y = jnp.dot(bu_scr[...].astype(BF16), cd_ref[...], preferred_element_type=F32) + d_ref[...] * xs
    y_ref[...] = jax.nn.gelu(y).astype(BF16)
    hf_ref[...] = st_scr[...]


def _ssm_core_call(xs, bd, cd, a_cat, d_skip, h0, *, nb, steps):
    n_rows, w = xs.shape
    packs, ch, ns2 = bd.shape
    nst = h0.shape[1]
    tr = steps * nb
    return pl.pallas_call(
        functools.partial(_ssm_core_kernel, nb=nb, col_chunk=min(ns2 // 2, 512)),
        grid=(packs, n_rows // tr),
        in_specs=[pl.BlockSpec((tr, ch), lambda c, t: (t, c)),
                  pl.BlockSpec((None, ch, ns2), lambda c, t: (c, 0, 0)),
                  pl.BlockSpec((None, ns2, ch), lambda c, t: (c, 0, 0)),
                  pl.BlockSpec((None, 1, ns2), lambda c, t: (c, 0, 0)),
                  pl.BlockSpec((1, ch), lambda c, t: (0, c)),
                  pl.BlockSpec((None, nst, ns2), lambda c, t: (c, 0, 0))],
        out_specs=[pl.BlockSpec((tr, ch), lambda c, t: (t, c)),
                   pl.BlockSpec((None, nst, ns2), lambda c, t: (c, 0, 0))],
        out_shape=[jax.ShapeDtypeStruct((n_rows, w), BF16),
                   jax.ShapeDtypeStruct((packs, nst, ns2), F32)],
        scratch_shapes=[pltpu.VMEM((tr, ns2), F32), pltpu.VMEM((nst, ns2), F32)],
        compiler_params=_params("arbitrary", "arbitrary"),
        name="ssm_core",
    )(xs, bd, cd, a_cat, d_skip.reshape(1, w), h0)


def _ssm_operands(f_re, f_im, a_re, a_im, b_re, b_im, c_re, c_im):
    g, p, k = b_re.shape
    pg = SSM_PACK_GROUPS
    packs = g // pg
    bb_re = f_re[..., None] * b_re - f_im[..., None] * b_im
    bb_im = f_re[..., None] * b_im + f_im[..., None] * b_re
    eye = jnp.eye(pg, dtype=F32)

    def bdiag(m):
        m = m.reshape(packs, pg, p, k).transpose(0, 1, 3, 2)
        return (m[:, :, :, None, :] * eye[None, :, None, :, None]).reshape(packs, pg * k, pg * p)

    def cdiag(m):
        m = m.reshape(packs, pg, k, p).transpose(0, 1, 3, 2)
        return (m[:, :, :, None, :] * eye[None, :, None, :, None]).reshape(packs, pg * p, pg * k)

    bd = jnp.concatenate([bdiag(bb_re), bdiag(bb_im)], axis=2).astype(BF16)
    cd = jnp.concatenate([cdiag(c_re), cdiag(-c_im)], axis=1).astype(BF16)
    a_cat = jnp.concatenate([a_re.reshape(packs, 1, pg * p), a_im.reshape(packs, 1, pg * p)], axis=2)
    return bd, cd, a_cat


def _pack_state(s_re, s_im):
    b, g, p = s_re.shape
    packs = g // SSM_PACK_GROUPS

    def one(s):
        s = s.reshape(b, packs, SSM_PACK_GROUPS * p).transpose(1, 0, 2)
        return s if b % ROW_GROUP == 0 else jnp.tile(s, (1, ROW_GROUP // b, 1))

    return jnp.concatenate([one(s_re), one(s_im)], axis=2)


def _unpack_state(hf, b, g, p):
    ns = hf.shape[2] // 2

    def one(s):
        return s[:, :b].transpose(1, 0, 2).reshape(b, g, p)

    return one(hf[:, :, :ns]), one(hf[:, :, ns:])


PEER_CAND_ROWS = PEER_TOPK + (PEER_TOPK // 2 - 1) * (PEER_TOPK // 2) + PEER_TOPK // 2


def _peer_select_kernel(ht_ref, wq_ref, keys_ref, c0_ref, e0_ref, a1_ref, e1_ref, top_scr, cand_scr):
    dk = keys_ref.shape[2]
    kk, hk = PEER_TOPK, PEER_TOPK // 2
    q = jnp.dot(wq_ref[...], ht_ref[...], preferred_element_type=F32)
    scores = []
    for z in range(2):
        s = jnp.dot(keys_ref[z], q[z * dk:(z + 1) * dk], preferred_element_type=F32,
                    precision=lax.Precision.HIGHEST)
        scores.append(s)
        cur = s
        for k in range(kk):
            m = jnp.max(cur, axis=0, keepdims=True)
            top_scr[z, k:k + 1, :] = m
            cur = jnp.where(cur == m, -jnp.inf, cur)
    cand_scr[0:kk, :] = top_scr[0, 0:1, :] + top_scr[1]
    for i in range(1, hk):
        cand_scr[kk + (i - 1) * hk:kk + i * hk, :] = top_scr[0, i:i + 1, :] + top_scr[1, 0:hk, :]
    cand_scr[kk + (hk - 1) * hk:, :] = top_scr[0, hk:kk, :] + top_scr[1, 0:1, :]
    cur = cand_scr[...]
    best = top_scr[0, 0:1, :] + top_scr[1, 0:1, :]
    zsum = jnp.zeros_like(best)
    thr = best
    for k in range(kk):
        thr = jnp.max(cur, axis=0, keepdims=True)
        zsum = zsum + jnp.exp(thr - best)
        cur = jnp.where(cur == thr, -jnp.inf, cur)
    s0, s1 = scores
    c0 = jnp.full(s0.shape, jnp.inf, F32)
    for j in range(kk):
        bj = top_scr[1, j:j + 1, :]
        c0 = jnp.where(s0 + bj >= thr, bj, c0)
    c0_ref[...] = jnp.where(s0 >= top_scr[0, kk - 1:kk, :], c0, jnp.inf)
    e0_ref[...] = jnp.exp(s0 - top_scr[0, 0:1, :]) / zsum
    a1_ref[...] = s1
    e1_ref[...] = jnp.exp(s1 - top_scr[1, 0:1, :])


def _peer_select_call(ht, wq_t, keys, layer):
    d, t = ht.shape
    heads, _, n_keys, dk = keys.shape[1:]
    tt = _tile(t, 512)
    sc_spec = pl.BlockSpec((None, n_keys, tt), lambda i, h: (h, 0, i))
    sc_shape = jax.ShapeDtypeStruct((heads, n_keys, t), F32)
    return pl.pallas_call(
        _peer_select_kernel,
        grid=(t // tt, heads),
        in_specs=[pl.BlockSpec((d, tt), lambda i, h: (0, i)),
                  pl.BlockSpec((None, 2 * dk, d), lambda i, h: (layer, h, 0)),
                  pl.BlockSpec((None, None, 2, n_keys, dk), lambda i, h: (layer, h, 0, 0, 0))],
        out_specs=[sc_spec] * 4,
        out_shape=[sc_shape] * 4,
        scratch_shapes=[pltpu.VMEM((2, PEER_TOPK, tt), F32),
                        pltpu.VMEM((PEER_CAND_ROWS, tt), F32)],
        compiler_params=_params("arbitrary", "arbitrary"),
        name="peer_select",
    )(ht, wq_t, keys)


def _peer_dense_kernel(ht_ref, u_ref, vt_ref, c0t_ref, e0t_ref, a1_ref, e1_ref, o_ref, act_scr, p_scr,
                       *, lane_chunk):
    nsub, heads, tt = c0t_ref.shape
    n2 = a1_ref.shape[1]
    j = pl.program_id(1)
    slot = j % 2

    @pl.when(j == 0)
    def _():
        o_ref[...] = jnp.zeros_like(o_ref)
        act_scr[1] = jnp.zeros(act_scr.shape[1:], F32)

    for r in range(nsub):
        rows = slice(r * n2, (r + 1) * n2)
        for lc in range(tt // lane_chunk):
            cols = slice(lc * lane_chunk, (lc + 1) * lane_chunk)
            w = jnp.zeros((n2, lane_chunk), F32)
            for h in range(heads):
                hit = a1_ref[h, :, cols] >= c0t_ref[r, h:h + 1, cols]
                w = w + jnp.where(hit, e0t_ref[r, h:h + 1, cols] * e1_ref[h, :, cols], 0.0)
            p_scr[rows, cols] = (w * act_scr[1 - slot, rows, cols]).astype(BF16)
    act_scr[slot] = jax.nn.gelu(jnp.dot(u_ref[...], ht_ref[...], preferred_element_type=F32))
    o_ref[...] += jnp.dot(vt_ref[...], p_scr[...], preferred_element_type=F32)


def _peer_dense_call(ht, u_bf, vt_bf, c0t, e0t, a1, e1, layer, *, ec=512):
    d, t = ht.shape
    n_exp = u_bf.shape[1]
    heads, n2, _ = a1.shape
    tt = _tile(t, 512)
    ec = _tile(n_exp, ec)
    nsub = ec // n2
    n_chunks = n_exp // ec
    cur = lambda j: jnp.minimum(j, n_chunks - 1)
    prev = lambda j: jnp.maximum(j - 1, 0)
    row_spec = pl.BlockSpec((nsub, heads, tt), lambda i, j: (prev(j), 0, i))
    once = pl.Buffered(1)
    full_spec = pl.BlockSpec((heads, n2, tt), lambda i, j: (0, 0, i), pipeline_mode=once)
    return pl.pallas_call(
        functools.partial(_peer_dense_kernel, lane_chunk=min(tt, 256)),
        grid=(t // tt, n_chunks + 1),
        in_specs=[pl.BlockSpec((d, tt), lambda i, j: (0, i), pipeline_mode=once),
                  pl.BlockSpec((None, ec, d), lambda i, j: (layer, cur(j), 0)),
                  pl.BlockSpec((None, d, ec), lambda i, j: (layer, 0, prev(j))),
                  row_spec, row_spec, full_spec, full_spec],
        out_specs=pl.BlockSpec((d, tt), lambda i, j: (0, i)),
        out_shape=jax.ShapeDtypeStruct((d, t), F32),
        scratch_shapes=[pltpu.VMEM((2, ec, tt), F32), pltpu.VMEM((ec, tt), BF16)],
        compiler_params=_params("arbitrary", "arbitrary"),
        name="peer_dense",
    )(ht, u_bf, vt_bf, c0t, e0t, a1, e1)


def _peer_ffn_t(ht, wq_t, keys, u_bf, vt_bf, layer):
    c0, e0, a1, e1 = _peer_select_call(ht, wq_t, keys, layer)
    return _peer_dense_call(ht, u_bf, vt_bf, c0.transpose(1, 0, 2), e0.transpose(1, 0, 2), a1, e1, layer)


def kernel(x_prompt, x_sample, c_prompt, c_sample, state_ssm_re, state_ssm_im, ada_w, ada_b, norm_mix_g, norm_ffn_g, final_g, gm_w_in, gm_b_in, gm_v_g, gm_w_s, gm_b_s, gm_w_out, ssm_w_in, ssm_lam_re, ssm_lam_im, ssm_log_dt, ssm_b_re, ssm_b_im, ssm_c_re, ssm_c_im, ssm_d, ssm_w_out, peer_w_q, peer_keys, peer_u, peer_v):
    bp, seq, d = x_prompt.shape
    bs, dseq, _ = x_sample.shape
    depth = ada_w.shape[0]
    tp, ts = bp * seq, bs * dseq
    chunk = gm_w_s.shape[2]
    hw = gm_w_in.shape[2] // 2
    n_groups, n_state = ssm_lam_re.shape[1:]
    w_ssm = ssm_w_in.shape[2]
    assert dseq == ROW_GROUP and seq % chunk == 0 and ts % chunk == 0
    assert bp * 2 == ROW_GROUP and bs % ROW_GROUP == 0
    groups = (bp, seq, bs)

    x = jnp.concatenate([x_prompt.reshape(tp, d), x_sample.reshape(ts, d)], axis=0)

    m_real = bp + bs
    m_pad = -(-m_real // ROW_GROUP) * ROW_GROUP
    c_all = jnp.pad(jnp.concatenate([c_prompt, c_sample], axis=0), ((0, m_pad - m_real), (0, 0)))
    mods = _ada_call(c_all, ada_w, ada_b)

    gm_w_in_bf, gm_w_out_bf = gm_w_in.astype(BF16), gm_w_out.astype(BF16)
    ssm_w_in_bf, ssm_w_out_bf = ssm_w_in.astype(BF16), ssm_w_out.astype(BF16)
    wq_t = peer_w_q.transpose(0, 2, 1).astype(BF16)
    u_bf = peer_u.astype(BF16)
    vt_bf = peer_v.transpose(0, 2, 1).astype(BF16)

    tril = jnp.tril(jnp.ones((chunk, chunk), F32))
    rep = chunk // dseq
    tril_s = jnp.tril(jnp.ones((dseq, dseq), F32))

    def spatial_operands(ia):
        w_p = gm_w_s[ia] * tril
        w_s = jnp.einsum("ab,gts->gatbs", jnp.eye(rep, dtype=F32),
                         gm_w_s[ia][:, :dseq, :dseq] * tril_s).reshape(-1, chunk, chunk)
        b_p = gm_b_s[ia][:, :, None]
        b_s = jnp.tile(gm_b_s[ia][:, :dseq], (1, rep))[:, :, None]
        return jnp.stack([w_p, w_s]).astype(BF16), jnp.stack([b_p, b_s])

    new_v, st_p, st_s = [], [], []
    peer_t = None
    me_prev = None
    ia = ib = 0
    for i in range(depth):
        me = _group_rows(mods[i], groups)
        if i == 0:
            (h,) = _resnorm_call(x, norm_mix_g[i], groups, me, sc_col=1, sh_col=0)
        else:
            x, h = _resnorm_call(x, norm_mix_g[i], groups, me, delta=peer_t, delta_t=True,
                                 me_gate=me_prev, gate_col=5, sc_col=1, sh_col=0)
        if i % 2 == 0:
            u = _mm_call(h, gm_w_in_bf, ia, bias=gm_b_in, act="gelu", out_dtype=BF16, n_out=hw)
            v_raw = _mm_call(h, gm_w_in_bf, ia, bias=gm_b_in, act="gelu", col_start=hw, n_out=hw)
            wmix, bmix = spatial_operands(ia)
            gated, v_s = _gmlp_gate_call(u, v_raw, gm_v_g[ia], wmix, bmix, tp // chunk)
            new_v.append(v_s.reshape(bs, dseq, hw))
            mix = _mm_call(gated, gm_w_out_bf, ia, tm=512, tn=512)
            ia += 1
        else:
            a_re, a_im, f_re, f_im = _ssm_disc_call(ssm_lam_re[ib], ssm_lam_im[ib], ssm_log_dt[ib])
            bd, cd, a_cat = _ssm_operands(f_re, f_im, a_re, a_im, ssm_b_re[ib], ssm_b_im[ib],
                                          ssm_c_re[ib], ssm_c_im[ib])
            xs_p = _mm_call(h, ssm_w_in_bf, ib, m_rows=tp, seq_major=(bp, seq)).reshape(tp, w_ssm)
            zero = jnp.zeros((bp, n_groups, n_state), F32)
            y_p, hf_p = _ssm_core_call(xs_p, bd, cd, a_cat, ssm_d[ib], _pack_state(zero, zero),
                                       nb=bp, steps=_tile(seq, 256))
            mix_p = _glu_call(y_p.reshape(seq, bp * w_ssm), ssm_w_out_bf, ib, seq_major=(bp, seq))
            xs_s = _mm_call(h, ssm_w_in_bf, ib, row_start=tp, m_rows=ts)
            xs_s = xs_s.reshape(bs, dseq, w_ssm).transpose(1, 0, 2).reshape(ts, w_ssm)
            y_s, hf_s = _ssm_core_call(xs_s, bd, cd, a_cat, ssm_d[ib],
                                       _pack_state(state_ssm_re[ib], state_ssm_im[ib]), nb=bs, steps=dseq)
            y_s = y_s.reshape(dseq, bs, w_ssm).transpose(1, 0, 2).reshape(ts, w_ssm)
            mix = jnp.concatenate([mix_p, _glu_call(y_s, ssm_w_out_bf, ib)], axis=0)
            st_p.append(_unpack_state(hf_p, bp, n_groups, n_state))
            st_s.append(_unpack_state(hf_s, bs, n_groups, n_state))
            ib += 1
        x, ht = _resnorm_call(x, norm_ffn_g[i], groups, me, delta=mix, me_gate=me, gate_col=2, sc_col=4,
                              sh_col=3, emit_h=False, emit_ht=True)
        peer_t = _peer_ffn_t(ht, wq_t, peer_keys, u_bf, vt_bf, i)
        me_prev = me
    (y,) = _resnorm_call(x, final_g, groups, delta=peer_t, delta_t=True, me_gate=me_prev, gate_col=5,
                         final=True)

    y_prompt = y[:tp].reshape(bp, seq, d)
    y_sample = y[tp:].reshape(bs, dseq, d)
    return (y_prompt, y_sample, jnp.stack(new_v),
            jnp.stack([s[0] for s in st_p]), jnp.stack([s[1] for s in st_p]),
            jnp.stack([s[0] for s in st_s]), jnp.stack([s[1] for s in st_s]))
```

```python
import functools
import math

import jax
import jax.numpy as jnp
from jax import lax
from jax.experimental import pallas as pl
from jax.experimental.pallas import tpu as pltpu

EPS = 1e-6
PEER_TOPK = 16
ROW_GROUP = 8
SSM_PACK_GROUPS = 16
VMEM_LIMIT_BYTES = 56 * 1024 * 1024
F32 = jnp.float32
BF16 = jnp.bfloat16


def _params(*sem):
    return pltpu.CompilerParams(dimension_semantics=sem, vmem_limit_bytes=VMEM_LIMIT_BYTES)


def _tile(n, pref, align=128):
    if n <= pref:
        return n
    for cand in range(pref - pref % align, 0, -align):
        if n % cand == 0:
            return cand
    raise ValueError((n, pref, align))


def _ada_kernel(c_ref, w_ref, b_ref, o_ref):
    c = c_ref[...]
    act = (c * (1.0 / (1.0 + jnp.exp(-c)))).astype(BF16)
    o_ref[...] = jnp.dot(act, w_ref[...].astype(BF16), preferred_element_type=F32) + b_ref[...]


def _ada_call(c_all, ada_w, ada_b):
    depth, d, n = ada_w.shape
    m = c_all.shape[0]
    tn = _tile(n, 512)
    return pl.pallas_call(
        _ada_kernel,
        grid=(depth, n // tn),
        in_specs=[pl.BlockSpec((m, d), lambda l, j: (0, 0)),
                  pl.BlockSpec((None, d, tn), lambda l, j: (l, 0, j)),
                  pl.BlockSpec((None, 1, tn), lambda l, j: (l, 0, j))],
        out_specs=pl.BlockSpec((None, m, tn), lambda l, j: (l, 0, j)),
        out_shape=jax.ShapeDtypeStruct((depth, m, n), F32),
        compiler_params=_params("arbitrary", "arbitrary"),
        name="ada_mod",
    )(c_all, ada_w, ada_b.reshape(depth, 1, n))


def _resnorm_kernel(*refs, x_split, has_delta, delta_t, delta_split, emit_h, emit_ht, final,
                    n_prompt_tiles):
    it = iter(refs)
    x_ref = next(it)
    xs_ref = next(it) if x_split else None
    in_prompt = pl.program_id(0) < n_prompt_tiles
    if has_delta:
        delta_ref = next(it)
        delta_s_ref = next(it) if delta_split else None
        gate_ref = next(it)
    g_ref = next(it)
    if not final:
        sc_ref = next(it)
        sh_ref = next(it)
    tm, d = x_ref.shape
    r = tm // ROW_GROUP
    x = x_ref[...]
    if x_split:
        x = jnp.where(in_prompt, x, xs_ref[...])
    x3 = x.reshape(r, ROW_GROUP, d)
    if has_delta:
        delta = delta_ref[...]
        if delta_split:
            delta = jnp.where(in_prompt, delta, delta_s_ref[...])
        if delta_t:
            delta = delta.T
        x3 = x3 + gate_ref[...][:, None, :] * delta.reshape(r, ROW_GROUP, d)
    var = jnp.mean(x3 * x3, axis=-1, keepdims=True)
    nrm = x3 * lax.rsqrt(var + EPS) * g_ref[...]
    if final:
        yp_ref, ys_ref = next(it), next(it)

        @pl.when(in_prompt)
        def _():
            yp_ref[...] = nrm.reshape(tm, d)

        @pl.when(jnp.logical_not(in_prompt))
        def _():
            ys_ref[...] = nrm.reshape(tm, d)

        return
    h = (nrm * (1.0 + sc_ref[...][:, None, :]) + sh_ref[...][:, None, :]).reshape(tm, d)
    if has_delta:
        next(it)[...] = x3.reshape(tm, d)
    if emit_h:
        next(it)[...] = h.astype(BF16)
    if emit_ht:
        next(it)[...] = h.T.astype(BF16)


def _resnorm_tile(groups):
    bp, seq, bs = groups
    return _tile(math.gcd(seq, bs * ROW_GROUP), 256)


def _group_rows(mod, groups):
    bp, _, bs = groups
    r = _resnorm_tile(groups) // ROW_GROUP
    return jnp.concatenate([jnp.repeat(mod[:bp], r, axis=0), mod[bp:bp + bs]], axis=0)


def _resnorm_call(x, gain, groups, me=None, *, delta=None, delta_t=False, me_gate=None, gate_col=None,
                  sc_col=None, sh_col=None, emit_h=True, emit_ht=False, final=False):
    x_split = isinstance(x, tuple)
    d = x[0].shape[1] if x_split else x.shape[1]
    bp, seq, bs = groups
    t = bp * seq + bs * ROW_GROUP
    tm = _resnorm_tile(groups)
    r = tm // ROW_GROUP
    tps = seq // tm
    n_prompt_tiles = bp * tps
    has_delta = delta is not None
    row = pl.BlockSpec((tm, d), lambda i: (i, 0))
    col = pl.BlockSpec((d, tm), lambda i: (0, i))

    def me_spec(k):
        return pl.BlockSpec(
            (r, d), lambda i, k=k: (jnp.where(i < n_prompt_tiles, i // tps, bp + i - n_prompt_tiles), k))

    row_p = pl.BlockSpec((tm, d), lambda i: (jnp.minimum(i, n_prompt_tiles - 1), 0))
    row_s = pl.BlockSpec((tm, d), lambda i: (jnp.maximum(i - n_prompt_tiles, 0), 0))
    delta_split = isinstance(delta, tuple)
    in_specs, args = ([row_p, row_s], list(x)) if x_split else ([row], [x])
    if has_delta:
        if delta_split:
            assert not delta_t
            in_specs += [row_p, row_s]
            args += list(delta)
        else:
            in_specs.append(col if delta_t else row)
            args.append(delta)
        in_specs.append(me_spec(gate_col))
        args.append(me_gate)
    in_specs.append(pl.BlockSpec((1, d), lambda i: (0, 0)))
    args.append(gain.reshape(1, d))
    if not final:
        in_specs += [me_spec(sc_col), me_spec(sh_col)]
        args += [me, me]
    out_specs, out_shape = [], []
    if final:
        out_specs += [row_p, row_s]
        out_shape += [jax.ShapeDtypeStruct((n_prompt_tiles * tm, d), F32),
                      jax.ShapeDtypeStruct((t - n_prompt_tiles * tm, d), F32)]
    elif has_delta:
        out_specs.append(row)
        out_shape.append(jax.ShapeDtypeStruct((t, d), F32))
    if not final and emit_h:
        out_specs.append(row)
        out_shape.append(jax.ShapeDtypeStruct((t, d), BF16))
    if not final and emit_ht:
        out_specs.append(col)
        out_shape.append(jax.ShapeDtypeStruct((d, t), BF16))
    return pl.pallas_call(
        functools.partial(_resnorm_kernel, x_split=x_split, has_delta=has_delta, delta_t=delta_t,
                          delta_split=delta_split,
                          emit_h=emit_h, emit_ht=emit_ht, final=final, n_prompt_tiles=n_prompt_tiles),
        grid=(t // tm,),
        in_specs=in_specs, out_specs=out_specs, out_shape=out_shape,
        compiler_params=_params("arbitrary"),
        name="resnorm",
    )(*args)


def _mm_kernel(a_ref, b_ref, *rest, has_bias, act, row_block=512):
    o_ref = rest[-1]
    tm = a_ref.shape[0]
    rb = row_block if (has_bias or act) and tm % row_block == 0 else tm
    for r0 in range(0, tm, rb):
        acc = jnp.dot(a_ref[r0:r0 + rb, :], b_ref[...], preferred_element_type=F32)
        if has_bias:
            acc = acc + rest[0][...]
        if act == "gelu":
            acc = jax.nn.gelu(acc)
        o_ref[r0:r0 + rb, :] = acc.astype(o_ref.dtype)


def _mm_call(a, b, layer, *, bias=None, act=None, out_dtype=F32, col_start=0, n_out=None,
             row_start=0, m_rows=None, seq_major=None, tm=1536, tn=512):
    k = a.shape[1]
    m = a.shape[0] if m_rows is None else m_rows
    n_out = b.shape[2] if n_out is None else n_out
    tm = _tile(m if seq_major is None else seq_major[1], tm)
    tn = _tile(n_out, tn)
    assert col_start % tn == 0 and row_start % tm == 0
    off, roff, nj = col_start // tn, row_start // tm, n_out // tn
    in_specs = [pl.BlockSpec((tm, k), lambda i, j: (i + roff, 0)),
                pl.BlockSpec((None, k, tn), lambda i, j: (layer, 0, j + off))]
    args = [a, b]
    if bias is not None:
        in_specs.append(pl.BlockSpec((None, 1, tn), lambda i, j: (layer, 0, j + off)))
        args.append(bias.reshape(bias.shape[0], 1, -1))
    if seq_major is None:
        out_spec = pl.BlockSpec((tm, tn), lambda i, j: (i, j))
        out_shape = jax.ShapeDtypeStruct((m, n_out), out_dtype)
    else:
        n_seq, seq_len = seq_major
        tps = seq_len // tm
        out_spec = pl.BlockSpec((tm, tn), lambda i, j: (i % tps, (i // tps) * nj + j))
        out_shape = jax.ShapeDtypeStruct((seq_len, n_seq * n_out), out_dtype)
    return pl.pallas_call(
        functools.partial(_mm_kernel, has_bias=bias is not None, act=act),
        grid=(m // tm, nj),
        in_specs=in_specs,
        out_specs=out_spec,
        out_shape=out_shape,
        compiler_params=_params("arbitrary", "arbitrary"),
        name="matmul",
    )(*args)


def _glu_kernel(a_ref, ba_ref, bg_ref, o_ref):
    a = a_ref[...]
    lin = jnp.dot(a, ba_ref[...], preferred_element_type=F32)
    gat = jnp.dot(a, bg_ref[...], preferred_element_type=F32)
    o_ref[...] = lin * (1.0 / (1.0 + jnp.exp(-gat)))


def _glu_call(a, b, layer, *, seq_major=None, tm=1024, tn=256):
    k = b.shape[1]
    n = b.shape[2] // 2
    if seq_major is None:
        m = a.shape[0]
        tm = _tile(m, tm)
        a_spec = pl.BlockSpec((tm, k), lambda i, j: (i, 0))
    else:
        n_seq, seq_len = seq_major
        m = n_seq * seq_len
        tm = _tile(seq_len, tm)
        tps = seq_len // tm
        a_spec = pl.BlockSpec((tm, k), lambda i, j: (i % tps, i // tps))
    tn = _tile(n, tn)
    off = n // tn
    return pl.pallas_call(
        _glu_kernel,
        grid=(m // tm, n // tn),
        in_specs=[a_spec,
                  pl.BlockSpec((None, k, tn), lambda i, j: (layer, 0, j)),
                  pl.BlockSpec((None, k, tn), lambda i, j: (layer, 0, j + off))],
        out_specs=pl.BlockSpec((tm, tn), lambda i, j: (i, j)),
        out_shape=jax.ShapeDtypeStruct((m, n), F32),
        compiler_params=_params("arbitrary", "arbitrary"),
        name="glu_matmul",
    )(a, b, b)


def _gmlp_gate_kernel(u_ref, v_ref, vg_ref, w_ref, b_ref, o_ref, vs_ref, *, groups, n_prompt_tiles):
    v = v_ref[...]
    var = jnp.mean(v * v, axis=-1, keepdims=True)
    vn = v * lax.rsqrt(var + EPS) * vg_ref[...]

    @pl.when(pl.program_id(0) >= n_prompt_tiles)
    def _():
        vs_ref[...] = vn

    gdim = v.shape[1] // groups
    for g in range(groups):
        sl = slice(g * gdim, (g + 1) * gdim)
        s = jnp.dot(w_ref[g], vn[:, sl].astype(BF16), preferred_element_type=F32) + b_ref[g]
        o_ref[:, sl] = (u_ref[:, sl].astype(F32) * s).astype(BF16)


def _gmlp_gate_call(u, v_raw, v_g, wmix, bmix, n_prompt_tiles):
    t, hw = u.shape
    _, groups, blk, _ = wmix.shape
    n_tiles = t // blk
    kind = lambda i: jnp.where(i >= n_prompt_tiles, 1, 0)
    return pl.pallas_call(
        functools.partial(_gmlp_gate_kernel, groups=groups, n_prompt_tiles=n_prompt_tiles),
        grid=(n_tiles,),
        in_specs=[pl.BlockSpec((blk, hw), lambda i: (i, 0)),
                  pl.BlockSpec((blk, hw), lambda i: (i, 0)),
                  pl.BlockSpec((1, hw), lambda i: (0, 0)),
                  pl.BlockSpec((None, groups, blk, blk), lambda i: (kind(i), 0, 0, 0)),
                  pl.BlockSpec((None, groups, blk, 1), lambda i: (kind(i), 0, 0, 0))],
        out_specs=[pl.BlockSpec((blk, hw), lambda i: (i, 0)),
                   pl.BlockSpec((blk, hw), lambda i: (jnp.maximum(i - n_prompt_tiles, 0), 0))],
        out_shape=[jax.ShapeDtypeStruct((t, hw), BF16),
                   jax.ShapeDtypeStruct(((n_tiles - n_prompt_tiles) * blk, hw), F32)],
        compiler_params=_params("arbitrary"),
        name="gmlp_gate",
    )(u, v_raw, v_g.reshape(1, hw), wmix, bmix)


def _ssm_disc_kernel(lre_ref, lim_ref, ldt_ref, are_ref, aim_ref, fre_ref, fim_ref):
    lre = lre_ref[...]
    lim = lim_ref[...]
    dt = jnp.exp(ldt_ref[...])
    mag = jnp.exp(lre * dt)
    ang = lim * dt
    a_re = mag * jnp.cos(ang)
    a_im = mag * jnp.sin(ang)
    n_re = a_re - 1
    den = lre * lre + lim * lim
    are_ref[...] = a_re
    aim_ref[...] = a_im
    fre_ref[...] = (n_re * lre + a_im * lim) / den
    fim_ref[...] = (a_im * lre - n_re * lim) / den


def _ssm_disc_call(lam_re, lam_im, log_dt):
    g, p = lam_re.shape
    shp = jax.ShapeDtypeStruct((g, p), F32)
    return pl.pallas_call(_ssm_disc_kernel, out_shape=[shp] * 4, name="ssm_discretize")(
        lam_re, lam_im, log_dt.reshape(g, 1))


def _ssm_core_kernel(xs_ref, bd_ref, cd_ref, a_ref, d_ref, h0_ref, y_ref, hf_ref, bu_scr, st_scr,
                     *, nb, col_chunk):
    n_rows, ch = xs_ref.shape
    ns = a_ref.shape[1] // 2
    half = ROW_GROUP // 2

    @pl.when(pl.program_id(1) == 0)
    def _():
        st_scr[...] = h0_ref[...]

    xs = xs_ref[...]
    xs_bf = xs.astype(BF16)
    y = d_ref[...] * xs
    if nb % ROW_GROUP != 0:
        assert nb == half
        hi = lax.broadcasted_iota(jnp.int32, (ROW_GROUP, col_chunk), 0) >= half

    for cc in range(ns // col_chunk):
        cre = slice(cc * col_chunk, (cc + 1) * col_chunk)
        cim = slice(ns + cc * col_chunk, ns + (cc + 1) * col_chunk)
        a_re = jnp.broadcast_to(a_ref[:, cre], (ROW_GROUP, col_chunk))
        a_im = jnp.broadcast_to(a_ref[:, cim], (ROW_GROUP, col_chunk))
        bu_scr[:, cre] = jnp.dot(xs_bf, bd_ref[:, cre], preferred_element_type=F32)
        bu_scr[:, cim] = jnp.dot(xs_bf, bd_ref[:, cim], preferred_element_type=F32)

        if nb % ROW_GROUP == 0:
            for rb in range(nb // ROW_GROUP):
                srows = slice(rb * ROW_GROUP, (rb + 1) * ROW_GROUP)
                s_re, s_im = st_scr[srows, cre], st_scr[srows, cim]
                for l in range(n_rows // nb):
                    rows = slice(l * nb + rb * ROW_GROUP, l * nb + (rb + 1) * ROW_GROUP)
                    s_re, s_im = (a_re * s_re - a_im * s_im + bu_scr[rows, cre],
                                  a_re * s_im + a_im * s_re + bu_scr[rows, cim])
                    bu_scr[rows, cre] = s_re
                    bu_scr[rows, cim] = s_im
                st_scr[srows, cre] = s_re
                st_scr[srows, cim] = s_im
        else:
            ahi_re = jnp.where(hi, a_re, 0.0)
            ahi_im = jnp.where(hi, a_im, 0.0)
            ap_re = jnp.where(hi, a_re * a_re - a_im * a_im, a_re)
            ap_im = jnp.where(hi, 2.0 * a_re * a_im, a_im)
            c_re, c_im = st_scr[:, cre], st_scr[:, cim]
            for s in range(n_rows // ROW_GROUP):
                rows = slice(s * ROW_GROUP, (s + 1) * ROW_GROUP)
                v_re = bu_scr[rows, cre]
                v_im = bu_scr[rows, cim]
                r_re = pltpu.roll(v_re, half, 0)
                r_im = pltpu.roll(v_im, half, 0)
                o_re = ap_re * c_re - ap_im * c_im + (v_re + (ahi_re * r_re - ahi_im * r_im))
                o_im = ap_re * c_im + ap_im * c_re + (v_im + (ahi_re * r_im + ahi_im * r_re))
                bu_scr[rows, cre] = o_re
                bu_scr[rows, cim] = o_im
                c_re = jnp.where(hi, o_re, pltpu.roll(o_re, half, 0))
                c_im = jnp.where(hi, o_im, pltpu.roll(o_im, half, 0))
            st_scr[:, cre] = c_re
            st_scr[:, cim] = c_im

        y = y + jnp.dot(bu_scr[:, cre].astype(BF16), cd_ref[cre, :], preferred_element_type=F32)
        y = y + jnp.dot(bu_scr[:, cim].astype(BF16), cd_ref[cim, :], preferred_element_type=F32)
    y_ref[...] = jax.nn.gelu(y).astype(BF16)
    hf_ref[...] = st_scr[...]


def _ssm_core_call(xs, bd, cd, a_cat, d_skip, h0, *, nb, steps):
    n_rows, w = xs.shape
    packs, ch, ns2 = bd.shape
    nst = h0.shape[1]
    tr = steps * nb
    return pl.pallas_call(
        functools.partial(_ssm_core_kernel, nb=nb, col_chunk=min(ns2 // 2, 256)),
        grid=(packs, n_rows // tr),
        in_specs=[pl.BlockSpec((tr, ch), lambda c, t: (t, c)),
                  pl.BlockSpec((None, ch, ns2), lambda c, t: (c, 0, 0)),
                  pl.BlockSpec((None, ns2, ch), lambda c, t: (c, 0, 0)),
                  pl.BlockSpec((None, 1, ns2), lambda c, t: (c, 0, 0)),
                  pl.BlockSpec((1, ch), lambda c, t: (0, c)),
                  pl.BlockSpec((None, nst, ns2), lambda c, t: (c, 0, 0))],
        out_specs=[pl.BlockSpec((tr, ch), lambda c, t: (t, c)),
                   pl.BlockSpec((None, nst, ns2), lambda c, t: (c, 0, 0))],
        out_shape=[jax.ShapeDtypeStruct((n_rows, w), BF16),
                   jax.ShapeDtypeStruct((packs, nst, ns2), F32)],
        scratch_shapes=[pltpu.VMEM((tr, ns2), F32), pltpu.VMEM((nst, ns2), F32)],
        compiler_params=_params("arbitrary", "arbitrary"),
        name="ssm_core",
    )(xs, bd, cd, a_cat, d_skip.reshape(1, w), h0)


def _ssm_operands(f_re, f_im, a_re, a_im, b_re, b_im, c_re, c_im):
    g, p, k = b_re.shape
    pg = SSM_PACK_GROUPS
    packs = g // pg
    bb_re = f_re[..., None] * b_re - f_im[..., None] * b_im
    bb_im = f_re[..., None] * b_im + f_im[..., None] * b_re
    eye = jnp.eye(pg, dtype=F32)

    def bdiag(m):
        m = m.reshape(packs, pg, p, k).transpose(0, 1, 3, 2)
        return (m[:, :, :, None, :] * eye[None, :, None, :, None]).reshape(packs, pg * k, pg * p)

    def cdiag(m):
        m = m.reshape(packs, pg, k, p).transpose(0, 1, 3, 2)
        return (m[:, :, :, None, :] * eye[None, :, None, :, None]).reshape(packs, pg * p, pg * k)

    bd = jnp.concatenate([bdiag(bb_re), bdiag(bb_im)], axis=2).astype(BF16)
    cd = jnp.concatenate([cdiag(c_re), cdiag(-c_im)], axis=1).astype(BF16)
    a_cat = jnp.concatenate([a_re.reshape(packs, 1, pg * p), a_im.reshape(packs, 1, pg * p)], axis=2)
    return bd, cd, a_cat


def _pack_state(s_re, s_im):
    b, g, p = s_re.shape
    packs = g // SSM_PACK_GROUPS

    def one(s):
        s = s.reshape(b, packs, SSM_PACK_GROUPS * p).transpose(1, 0, 2)
        return s if b % ROW_GROUP == 0 else jnp.tile(s, (1, ROW_GROUP // b, 1))

    return jnp.concatenate([one(s_re), one(s_im)], axis=2)


def _unpack_state(hf, b, g, p):
    ns = hf.shape[2] // 2

    def one(s):
        return s[:, :b].transpose(1, 0, 2).reshape(b, g, p)

    return one(hf[:, :, :ns]), one(hf[:, :, ns:])


PEER_CAND_ROWS = PEER_TOPK + (PEER_TOPK // 2 - 1) * (PEER_TOPK // 2) + PEER_TOPK // 2


def _peer_select_kernel(ht_ref, wq_ref, keys_ref, c0_ref, e0_ref, a1_ref, e1_ref, top_scr, cand_scr,
                        cs_scr):
    hps, _, _, dk = keys_ref.shape
    kk, hk = PEER_TOPK, PEER_TOPK // 2
    q = jnp.dot(wq_ref[...], ht_ref[...], preferred_element_type=F32)
    for hh in range(hps):
        top, cand, cs = top_scr.at[hh], cand_scr.at[hh], cs_scr.at[hh]
        scores = []
        for z in range(2):
            qz = q[(2 * hh + z) * dk:(2 * hh + z + 1) * dk]
            s = jnp.dot(keys_ref[hh, z], qz, preferred_element_type=F32,
                        precision=lax.Precision.HIGHEST)
            scores.append(s)
            cur = s
            for k in range(kk):
                m = jnp.max(cur, axis=0, keepdims=True)
                top[z, k:k + 1, :] = m
                cur = jnp.where(cur == m, -jnp.inf, cur)
        cand[0:kk, :] = top[0, 0:1, :] + top[1]
        for i in range(1, hk):
            cand[kk + (i - 1) * hk:kk + i * hk, :] = top[0, i:i + 1, :] + top[1, 0:hk, :]
        cand[kk + (hk - 1) * hk:, :] = top[0, hk:kk, :] + top[1, 0:1, :]
        cur = cand[...]
        best = top[0, 0:1, :] + top[1, 0:1, :]
        zsum = jnp.zeros_like(best)
        thr = best
        for k in range(kk):
            thr = jnp.max(cur, axis=0, keepdims=True)
            zsum = zsum + jnp.exp(thr - best)
            cur = jnp.where(cur == thr, -jnp.inf, cur)
        a_sorted = top[0]
        c_sorted = jnp.full(a_sorted.shape, jnp.inf, F32)
        for j in range(kk):
            bj = top[1, j:j + 1, :]
            c_sorted = jnp.where(a_sorted + bj >= thr, bj, c_sorted)
        cs[...] = c_sorted
        s0, s1 = scores
        c0 = jnp.full(s0.shape, jnp.inf, F32)
        for i in range(kk):
            c0 = jnp.where(s0 == top[0, i:i + 1, :], cs[i:i + 1, :], c0)
        c0_ref[hh] = c0
        e0_ref[hh] = jnp.exp(s0 - top[0, 0:1, :]) / zsum
        a1_ref[hh] = s1
        e1_ref[hh] = jnp.exp(s1 - top[1, 0:1, :])


def _peer_select_call(ht, wq_t, keys, layer):
    d, t = ht.shape
    heads, _, n_keys, dk = keys.shape[1:]
    tt = _tile(t, 512)
    hps = 2 if heads % 2 == 0 else 1
    sc_spec = pl.BlockSpec((hps, n_keys, tt), lambda i, h: (h, 0, i))
    sc_shape = jax.ShapeDtypeStruct((heads, n_keys, t), F32)
    return pl.pallas_call(
        _peer_select_kernel,
        grid=(t // tt, heads // hps),
        in_specs=[pl.BlockSpec((d, tt), lambda i, h: (0, i)),
                  pl.BlockSpec((None, hps * 2 * dk, d), lambda i, h: (layer, h, 0)),
                  pl.BlockSpec((None, hps, 2, n_keys, dk), lambda i, h: (layer, h, 0, 0, 0))],
        out_specs=[sc_spec] * 4,
        out_shape=[sc_shape] * 4,
        scratch_shapes=[pltpu.VMEM((hps, 2, PEER_TOPK, tt), F32),
                        pltpu.VMEM((hps, PEER_CAND_ROWS, tt), F32),
                        pltpu.VMEM((hps, PEER_TOPK, tt), F32)],
        compiler_params=_params("arbitrary", "arbitrary"),
        name="peer_select",
    )(ht, wq_t, keys)


def _peer_dense_kernel(ht_ref, u_ref, vt_ref, c0t_ref, e0t_ref, a1_ref, e1_ref, o_ref, act_scr, p_scr,
                       *, lane_chunk):
    nsub, heads, tt = c0t_ref.shape
    n2 = a1_ref.shape[1]
    j = pl.program_id(1)
    slot = j % 2

    @pl.when(j == 0)
    def _():
        o_ref[...] = jnp.zeros_like(o_ref)
        act_scr[1] = jnp.zeros(act_scr.shape[1:], F32)

    for r in range(nsub):
        rows = slice(r * n2, (r + 1) * n2)
        for lc in range(tt // lane_chunk):
            cols = slice(lc * lane_chunk, (lc + 1) * lane_chunk)
            w = jnp.zeros((n2, lane_chunk), F32)
            for h in range(heads):
                hit = a1_ref[h, :, cols] >= c0t_ref[r, h:h + 1, cols]
                w = w + jnp.where(hit, e0t_ref[r, h:h + 1, cols] * e1_ref[h, :, cols], 0.0)
            p_scr[rows, cols] = (w * act_scr[1 - slot, rows, cols]).astype(BF16)
    act_scr[slot] = jax.nn.gelu(jnp.dot(u_ref[...], ht_ref[...], preferred_element_type=F32))
    o_ref[...] += jnp.dot(vt_ref[...], p_scr[...], preferred_element_type=F32)


def _peer_dense_call(ht, u_bf, vt_bf, c0t, e0t, a1, e1, layer, *, ec=512):
    d, t = ht.shape
    n_exp = u_bf.shape[1]
    heads, n2, _ = a1.shape
    tt = _tile(t, 512)
    ec = _tile(n_exp, ec)
    nsub = ec // n2
    n_chunks = n_exp // ec
    cur = lambda j: jnp.minimum(j, n_chunks - 1)
    prev = lambda j: jnp.maximum(j - 1, 0)
    row_spec = pl.BlockSpec((nsub, heads, tt), lambda i, j: (prev(j), 0, i))
    once = pl.Buffered(1)
    full_spec = pl.BlockSpec((heads, n2, tt), lambda i, j: (0, 0, i), pipeline_mode=once)
    return pl.pallas_call(
        functools.partial(_peer_dense_kernel, lane_chunk=min(tt, 256)),
        grid=(t // tt, n_chunks + 1),
        in_specs=[pl.BlockSpec((d, tt), lambda i, j: (0, i), pipeline_mode=once),
                  pl.BlockSpec((None, ec, d), lambda i, j: (layer, cur(j), 0)),
                  pl.BlockSpec((None, d, ec), lambda i, j: (layer, 0, prev(j))),
                  row_spec, row_spec, full_spec, full_spec],
        out_specs=pl.BlockSpec((d, tt), lambda i, j: (0, i)),
        out_shape=jax.ShapeDtypeStruct((d, t), F32),
        scratch_shapes=[pltpu.VMEM((2, ec, tt), F32), pltpu.VMEM((ec, tt), BF16)],
        compiler_params=_params("arbitrary", "arbitrary"),
        name="peer_dense",
    )(ht, u_bf, vt_bf, c0t, e0t, a1, e1)


def _peer_ffn_t(ht, wq_t, keys, u_bf, vt_bf, layer):
    c0, e0, a1, e1 = _peer_select_call(ht, wq_t, keys, layer)
    return _peer_dense_call(ht, u_bf, vt_bf, c0.transpose(1, 0, 2), e0.transpose(1, 0, 2), a1, e1, layer)


def kernel(x_prompt, x_sample, c_prompt, c_sample, state_ssm_re, state_ssm_im, ada_w, ada_b, norm_mix_g, norm_ffn_g, final_g, gm_w_in, gm_b_in, gm_v_g, gm_w_s, gm_b_s, gm_w_out, ssm_w_in, ssm_lam_re, ssm_lam_im, ssm_log_dt, ssm_b_re, ssm_b_im, ssm_c_re, ssm_c_im, ssm_d, ssm_w_out, peer_w_q, peer_keys, peer_u, peer_v):
    bp, seq, d = x_prompt.shape
    bs, dseq, _ = x_sample.shape
    depth = ada_w.shape[0]
    tp, ts = bp * seq, bs * dseq
    chunk = gm_w_s.shape[2]
    hw = gm_w_in.shape[2] // 2
    n_groups, n_state = ssm_lam_re.shape[1:]
    w_ssm = ssm_w_in.shape[2]
    assert dseq == ROW_GROUP and seq % chunk == 0 and ts % chunk == 0
    assert bp * 2 == ROW_GROUP and bs % ROW_GROUP == 0
    groups = (bp, seq, bs)

    x = (x_prompt.reshape(tp, d), x_sample.reshape(ts, d))

    m_real = bp + bs
    m_pad = -(-m_real // ROW_GROUP) * ROW_GROUP
    c_all = jnp.pad(jnp.concatenate([c_prompt, c_sample], axis=0), ((0, m_pad - m_real), (0, 0)))
    mods = _ada_call(c_all, ada_w, ada_b)

    gm_w_in_bf, gm_w_out_bf = gm_w_in.astype(BF16), gm_w_out.astype(BF16)
    ssm_w_in_bf, ssm_w_out_bf = ssm_w_in.astype(BF16), ssm_w_out.astype(BF16)
    wq_t = peer_w_q.transpose(0, 2, 1).astype(BF16)
    u_bf = peer_u.astype(BF16)
    vt_bf = peer_v.transpose(0, 2, 1).astype(BF16)

    tril = jnp.tril(jnp.ones((chunk, chunk), F32))
    rep = chunk // dseq
    tril_s = jnp.tril(jnp.ones((dseq, dseq), F32))

    def spatial_operands(ia):
        w_p = gm_w_s[ia] * tril
        w_s = jnp.einsum("ab,gts->gatbs", jnp.eye(rep, dtype=F32),
                         gm_w_s[ia][:, :dseq, :dseq] * tril_s).reshape(-1, chunk, chunk)
        b_p = gm_b_s[ia][:, :, None]
        b_s = jnp.tile(gm_b_s[ia][:, :dseq], (1, rep))[:, :, None]
        return jnp.stack([w_p, w_s]).astype(BF16), jnp.stack([b_p, b_s])

    new_v, st_p, st_s = [], [], []
    peer_t = None
    me_prev = None
    ia = ib = 0
    for i in range(depth):
        me = _group_rows(mods[i], groups)
        if i == 0:
            (h,) = _resnorm_call(x, norm_mix_g[i], groups, me, sc_col=1, sh_col=0)
        else:
            x, h = _resnorm_call(x, norm_mix_g[i], groups, me, delta=peer_t, delta_t=True,
                                 me_gate=me_prev, gate_col=5, sc_col=1, sh_col=0)
        if i % 2 == 0:
            u = _mm_call(h, gm_w_in_bf, ia, bias=gm_b_in, act="gelu", out_dtype=BF16, n_out=hw)
            v_raw = _mm_call(h, gm_w_in_bf, ia, bias=gm_b_in, act="gelu", col_start=hw, n_out=hw)
            wmix, bmix = spatial_operands(ia)
            gated, v_s = _gmlp_gate_call(u, v_raw, gm_v_g[ia], wmix, bmix, tp // chunk)
            new_v.append(v_s.reshape(bs, dseq, hw))
            mix = _mm_call(gated, gm_w_out_bf, ia, tm=768, tn=512)
            ia += 1
        else:
            a_re, a_im, f_re, f_im = _ssm_disc_call(ssm_lam_re[ib], ssm_lam_im[ib], ssm_log_dt[ib])
            bd, cd, a_cat = _ssm_operands(f_re, f_im, a_re, a_im, ssm_b_re[ib], ssm_b_im[ib],
                                          ssm_c_re[ib], ssm_c_im[ib])
            xs_p = _mm_call(h, ssm_w_in_bf, ib, m_rows=tp, seq_major=(bp, seq)).reshape(tp, w_ssm)
            zero = jnp.zeros((bp, n_groups, n_state), F32)
            y_p, hf_p = _ssm_core_call(xs_p, bd, cd, a_cat, ssm_d[ib], _pack_state(zero, zero),
                                       nb=bp, steps=_tile(seq, 256))
            mix_p = _glu_call(y_p.reshape(seq, bp * w_ssm), ssm_w_out_bf, ib, seq_major=(bp, seq))
            xs_s = _mm_call(h, ssm_w_in_bf, ib, row_start=tp, m_rows=ts)
            xs_s = xs_s.reshape(bs, dseq, w_ssm).transpose(1, 0, 2).reshape(ts, w_ssm)
            y_s, hf_s = _ssm_core_call(xs_s, bd, cd, a_cat, ssm_d[ib],
                                       _pack_state(state_ssm_re[ib], state_ssm_im[ib]), nb=bs, steps=dseq)
            y_s = y_s.reshape(dseq, bs, w_ssm).transpose(1, 0, 2).reshape(ts, w_ssm)
            mix = (mix_p, _glu_call(y_s, ssm_w_out_bf, ib))
            st_p.append(_unpack_state(hf_p, bp, n_groups, n_state))
            st_s.append(_unpack_state(hf_s, bs, n_groups, n_state))
            ib += 1
        x, ht = _resnorm_call(x, norm_ffn_g[i], groups, me, delta=mix, me_gate=me, gate_col=2, sc_col=4,
                              sh_col=3, emit_h=False, emit_ht=True)
        peer_t = _peer_ffn_t(ht, wq_t, peer_keys, u_bf, vt_bf, i)
        me_prev = me
    y_p, y_s = _resnorm_call(x, final_g, groups, delta=peer_t, delta_t=True, me_gate=me_prev, gate_col=5,
                             final=True)

    y_prompt = y_p.reshape(bp, seq, d)
    y_sample = y_s.reshape(bs, dseq, d)
    return (y_prompt, y_sample, jnp.stack(new_v),
            jnp.stack([s[0] for s in st_p]), jnp.stack([s[1] for s in st_p]),
            jnp.stack([s[0] for s in st_s]), jnp.stack([s[1] for s in st_s]))
```

```python
import functools
import math

import jax
import jax.numpy as jnp
from jax import lax
from jax.experimental import pallas as pl
from jax.experimental.pallas import tpu as pltpu

EPS = 1e-6
PEER_TOPK = 16
ROW_GROUP = 8
SSM_PACK_GROUPS = 16
VMEM_LIMIT_BYTES = 56 * 1024 * 1024
F32 = jnp.float32
BF16 = jnp.bfloat16


def _params(*sem):
    return pltpu.CompilerParams(dimension_semantics=sem, vmem_limit_bytes=VMEM_LIMIT_BYTES)


def _tile(n, pref, align=128):
    if n <= pref:
        return n
    for cand in range(pref - pref % align, 0, -align):
        if n % cand == 0:
            return cand
    raise ValueError((n, pref, align))


def _ada_kernel(c_ref, w_ref, b_ref, o_ref):
    c = c_ref[...]
    act = (c * (1.0 / (1.0 + jnp.exp(-c)))).astype(BF16)
    o_ref[...] = jnp.dot(act, w_ref[...].astype(BF16), preferred_element_type=F32) + b_ref[...]


def _ada_call(c_all, ada_w, ada_b):
    depth, d, n = ada_w.shape
    m = c_all.shape[0]
    tn = _tile(n, 512)
    return pl.pallas_call(
        _ada_kernel,
        grid=(depth, n // tn),
        in_specs=[pl.BlockSpec((m, d), lambda l, j: (0, 0)),
                  pl.BlockSpec((None, d, tn), lambda l, j: (l, 0, j)),
                  pl.BlockSpec((None, 1, tn), lambda l, j: (l, 0, j))],
        out_specs=pl.BlockSpec((None, m, tn), lambda l, j: (l, 0, j)),
        out_shape=jax.ShapeDtypeStruct((depth, m, n), F32),
        compiler_params=_params("arbitrary", "arbitrary"),
        name="ada_mod",
    )(c_all, ada_w, ada_b.reshape(depth, 1, n))


def _resnorm_kernel(*refs, x_split, has_delta, delta_t, delta_split, emit_h, emit_ht, final,
                    n_prompt_tiles):
    it = iter(refs)
    x_ref = next(it)
    xs_ref = next(it) if x_split else None
    in_prompt = pl.program_id(0) < n_prompt_tiles
    if has_delta:
        delta_ref = next(it)
        delta_s_ref = next(it) if delta_split else None
        gate_ref = next(it)
    g_ref = next(it)
    if not final:
        sc_ref = next(it)
        sh_ref = next(it)
    tm, d = x_ref.shape
    r = tm // ROW_GROUP
    x = x_ref[...]
    if x_split:
        x = jnp.where(in_prompt, x, xs_ref[...])
    x3 = x.reshape(r, ROW_GROUP, d)
    if has_delta:
        delta = delta_ref[...]
        if delta_split:
            delta = jnp.where(in_prompt, delta, delta_s_ref[...])
        if delta_t:
            delta = delta.T
        x3 = x3 + gate_ref[...][:, None, :] * delta.reshape(r, ROW_GROUP, d)
    var = jnp.mean(x3 * x3, axis=-1, keepdims=True)
    nrm = x3 * lax.rsqrt(var + EPS) * g_ref[...]
    if final:
        yp_ref, ys_ref = next(it), next(it)

        @pl.when(in_prompt)
        def _():
            yp_ref[...] = nrm.reshape(tm, d)

        @pl.when(jnp.logical_not(in_prompt))
        def _():
            ys_ref[...] = nrm.reshape(tm, d)

        return
    h = (nrm * (1.0 + sc_ref[...][:, None, :]) + sh_ref[...][:, None, :]).reshape(tm, d)
    if has_delta:
        next(it)[...] = x3.reshape(tm, d)
    if emit_h:
        next(it)[...] = h.astype(BF16)
    if emit_ht:
        next(it)[...] = h.T.astype(BF16)


def _resnorm_tile(groups):
    bp, seq, bs = groups
    return _tile(math.gcd(seq, bs * ROW_GROUP), 256)


def _group_rows(mod, groups):
    bp, _, bs = groups
    r = _resnorm_tile(groups) // ROW_GROUP
    return jnp.concatenate([jnp.repeat(mod[:bp], r, axis=0), mod[bp:bp + bs]], axis=0)


def _resnorm_call(x, gain, groups, me=None, *, delta=None, delta_t=False, me_gate=None, gate_col=None,
                  sc_col=None, sh_col=None, emit_h=True, emit_ht=False, final=False):
    x_split = isinstance(x, tuple)
    d = x[0].shape[1] if x_split else x.shape[1]
    bp, seq, bs = groups
    t = bp * seq + bs * ROW_GROUP
    tm = _resnorm_tile(groups)
    r = tm // ROW_GROUP
    tps = seq // tm
    n_prompt_tiles = bp * tps
    has_delta = delta is not None
    row = pl.BlockSpec((tm, d), lambda i: (i, 0))
    col = pl.BlockSpec((d, tm), lambda i: (0, i))

    def me_spec(k):
        return pl.BlockSpec(
            (r, d), lambda i, k=k: (jnp.where(i < n_prompt_tiles, i // tps, bp + i - n_prompt_tiles), k))

    row_p = pl.BlockSpec((tm, d), lambda i: (jnp.minimum(i, n_prompt_tiles - 1), 0))
    row_s = pl.BlockSpec((tm, d), lambda i: (jnp.maximum(i - n_prompt_tiles, 0), 0))
    delta_split = isinstance(delta, tuple)
    in_specs, args = ([row_p, row_s], list(x)) if x_split else ([row], [x])
    if has_delta:
        if delta_split:
            assert not delta_t
            in_specs += [row_p, row_s]
            args += list(delta)
        else:
            in_specs.append(col if delta_t else row)
            args.append(delta)
        in_specs.append(me_spec(gate_col))
        args.append(me_gate)
    in_specs.append(pl.BlockSpec((1, d), lambda i: (0, 0)))
    args.append(gain.reshape(1, d))
    if not final:
        in_specs += [me_spec(sc_col), me_spec(sh_col)]
        args += [me, me]
    out_specs, out_shape = [], []
    if final:
        out_specs += [row_p, row_s]
        out_shape += [jax.ShapeDtypeStruct((n_prompt_tiles * tm, d), F32),
                      jax.ShapeDtypeStruct((t - n_prompt_tiles * tm, d), F32)]
    elif has_delta:
        out_specs.append(row)
        out_shape.append(jax.ShapeDtypeStruct((t, d), F32))
    if not final and emit_h:
        out_specs.append(row)
        out_shape.append(jax.ShapeDtypeStruct((t, d), BF16))
    if not final and emit_ht:
        out_specs.append(col)
        out_shape.append(jax.ShapeDtypeStruct((d, t), BF16))
    return pl.pallas_call(
        functools.partial(_resnorm_kernel, x_split=x_split, has_delta=has_delta, delta_t=delta_t,
                          delta_split=delta_split,
                          emit_h=emit_h, emit_ht=emit_ht, final=final, n_prompt_tiles=n_prompt_tiles),
        grid=(t // tm,),
        in_specs=in_specs, out_specs=out_specs, out_shape=out_shape,
        compiler_params=_params("arbitrary"),
        name="resnorm",
    )(*args)


def _mm_kernel(a_ref, b_ref, *rest, has_bias, act, row_block=512):
    o_ref = rest[-1]
    tm = a_ref.shape[0]
    rb = row_block if (has_bias or act) and tm % row_block == 0 else tm
    for r0 in range(0, tm, rb):
        acc = jnp.dot(a_ref[r0:r0 + rb, :], b_ref[...], preferred_element_type=F32)
        if has_bias:
            acc = acc + rest[0][...]
        if act == "gelu":
            acc = jax.nn.gelu(acc)
        o_ref[r0:r0 + rb, :] = acc.astype(o_ref.dtype)


def _mm_call(a, b, layer, *, bias=None, act=None, out_dtype=F32, col_start=0, n_out=None,
             row_start=0, m_rows=None, seq_major=None, tm=1536, tn=512):
    k = a.shape[1]
    m = a.shape[0] if m_rows is None else m_rows
    n_out = b.shape[2] if n_out is None else n_out
    tm = _tile(m if seq_major is None else seq_major[1], tm)
    tn = _tile(n_out, tn)
    assert col_start % tn == 0 and row_start % tm == 0
    off, roff, nj = col_start // tn, row_start // tm, n_out // tn
    in_specs = [pl.BlockSpec((tm, k), lambda i, j: (i + roff, 0)),
                pl.BlockSpec((None, k, tn), lambda i, j: (layer, 0, j + off))]
    args = [a, b]
    if bias is not None:
        in_specs.append(pl.BlockSpec((None, 1, tn), lambda i, j: (layer, 0, j + off)))
        args.append(bias.reshape(bias.shape[0], 1, -1))
    if seq_major is None:
        out_spec = pl.BlockSpec((tm, tn), lambda i, j: (i, j))
        out_shape = jax.ShapeDtypeStruct((m, n_out), out_dtype)
    else:
        n_seq, seq_len = seq_major
        tps = seq_len // tm
        out_spec = pl.BlockSpec((tm, tn), lambda i, j: (i % tps, (i // tps) * nj + j))
        out_shape = jax.ShapeDtypeStruct((seq_len, n_seq * n_out), out_dtype)
    return pl.pallas_call(
        functools.partial(_mm_kernel, has_bias=bias is not None, act=act),
        grid=(m // tm, nj),
        in_specs=in_specs,
        out_specs=out_spec,
        out_shape=out_shape,
        compiler_params=_params("arbitrary", "arbitrary"),
        name="matmul",
    )(*args)


def _glu_kernel(a_ref, ba_ref, bg_ref, o_ref):
    a = a_ref[...]
    lin = jnp.dot(a, ba_ref[...], preferred_element_type=F32)
    gat = jnp.dot(a, bg_ref[...], preferred_element_type=F32)
    o_ref[...] = lin * (1.0 / (1.0 + jnp.exp(-gat)))


def _glu_call(a, b, layer, *, seq_major=None, tm=1024, tn=256):
    k = b.shape[1]
    n = b.shape[2] // 2
    if seq_major is None:
        m = a.shape[0]
        tm = _tile(m, tm)
        a_spec = pl.BlockSpec((tm, k), lambda i, j: (i, 0))
    else:
        n_seq, seq_len = seq_major
        m = n_seq * seq_len
        tm = _tile(seq_len, tm)
        tps = seq_len // tm
        a_spec = pl.BlockSpec((tm, k), lambda i, j: (i % tps, i // tps))
    tn = _tile(n, tn)
    off = n // tn
    return pl.pallas_call(
        _glu_kernel,
        grid=(m // tm, n // tn),
        in_specs=[a_spec,
                  pl.BlockSpec((None, k, tn), lambda i, j: (layer, 0, j)),
                  pl.BlockSpec((None, k, tn), lambda i, j: (layer, 0, j + off))],
        out_specs=pl.BlockSpec((tm, tn), lambda i, j: (i, j)),
        out_shape=jax.ShapeDtypeStruct((m, n), F32),
        compiler_params=_params("arbitrary", "arbitrary"),
        name="glu_matmul",
    )(a, b, b)


def _gmlp_gate_kernel(u_ref, v_ref, vg_ref, w_ref, b_ref, o_ref, vs_ref, *, groups, n_prompt_tiles):
    v = v_ref[...]
    var = jnp.mean(v * v, axis=-1, keepdims=True)
    vn = v * lax.rsqrt(var + EPS) * vg_ref[...]

    @pl.when(pl.program_id(0) >= n_prompt_tiles)
    def _():
        vs_ref[...] = vn

    gdim = v.shape[1] // groups
    for g in range(groups):
        sl = slice(g * gdim, (g + 1) * gdim)
        s = jnp.dot(w_ref[g], vn[:, sl].astype(BF16), preferred_element_type=F32) + b_ref[g]
        o_ref[:, sl] = (u_ref[:, sl].astype(F32) * s).astype(BF16)


def _gmlp_gate_call(u, v_raw, v_g, wmix, bmix, n_prompt_tiles):
    t, hw = u.shape
    _, groups, blk, _ = wmix.shape
    n_tiles = t // blk
    kind = lambda i: jnp.where(i >= n_prompt_tiles, 1, 0)
    return pl.pallas_call(
        functools.partial(_gmlp_gate_kernel, groups=groups, n_prompt_tiles=n_prompt_tiles),
        grid=(n_tiles,),
        in_specs=[pl.BlockSpec((blk, hw), lambda i: (i, 0)),
                  pl.BlockSpec((blk, hw), lambda i: (i, 0)),
                  pl.BlockSpec((1, hw), lambda i: (0, 0)),
                  pl.BlockSpec((None, groups, blk, blk), lambda i: (kind(i), 0, 0, 0)),
                  pl.BlockSpec((None, groups, blk, 1), lambda i: (kind(i), 0, 0, 0))],
        out_specs=[pl.BlockSpec((blk, hw), lambda i: (i, 0)),
                   pl.BlockSpec((blk, hw), lambda i: (jnp.maximum(i - n_prompt_tiles, 0), 0))],
        out_shape=[jax.ShapeDtypeStruct((t, hw), BF16),
                   jax.ShapeDtypeStruct(((n_tiles - n_prompt_tiles) * blk, hw), F32)],
        compiler_params=_params("arbitrary"),
        name="gmlp_gate",
    )(u, v_raw, v_g.reshape(1, hw), wmix, bmix)


def _ssm_disc_kernel(lre_ref, lim_ref, ldt_ref, are_ref, aim_ref, fre_ref, fim_ref):
    lre = lre_ref[...]
    lim = lim_ref[...]
    dt = jnp.exp(ldt_ref[...])
    mag = jnp.exp(lre * dt)
    ang = lim * dt
    a_re = mag * jnp.cos(ang)
    a_im = mag * jnp.sin(ang)
    n_re = a_re - 1
    den = lre * lre + lim * lim
    are_ref[...] = a_re
    aim_ref[...] = a_im
    fre_ref[...] = (n_re * lre + a_im * lim) / den
    fim_ref[...] = (a_im * lre - n_re * lim) / den


def _ssm_disc_call(lam_re, lam_im, log_dt):
    g, p = lam_re.shape
    shp = jax.ShapeDtypeStruct((g, p), F32)
    return pl.pallas_call(_ssm_disc_kernel, out_shape=[shp] * 4, name="ssm_discretize")(
        lam_re, lam_im, log_dt.reshape(g, 1))


def _ssm_core_kernel(xs_ref, bd_ref, cd_ref, a_ref, d_ref, h0_ref, y_ref, hf_ref, bu_scr, st_scr,
                     *, nb, col_chunk):
    n_rows, ch = xs_ref.shape
    ns = a_ref.shape[1] // 2
    half = ROW_GROUP // 2

    @pl.when(pl.program_id(1) == 0)
    def _():
        st_scr[...] = h0_ref[...]

    xs = xs_ref[...]
    xs_bf = xs.astype(BF16)
    y = d_ref[...] * xs
    if nb % ROW_GROUP != 0:
        assert nb == half
        hi = lax.broadcasted_iota(jnp.int32, (ROW_GROUP, col_chunk), 0) >= half

    for cc in range(ns // col_chunk):
        cre = slice(cc * col_chunk, (cc + 1) * col_chunk)
        cim = slice(ns + cc * col_chunk, ns + (cc + 1) * col_chunk)
        a_re = jnp.broadcast_to(a_ref[:, cre], (ROW_GROUP, col_chunk))
        a_im = jnp.broadcast_to(a_ref[:, cim], (ROW_GROUP, col_chunk))
        bu_scr[:, cre] = jnp.dot(xs_bf, bd_ref[:, cre], preferred_element_type=F32)
        bu_scr[:, cim] = jnp.dot(xs_bf, bd_ref[:, cim], preferred_element_type=F32)

        if nb % ROW_GROUP == 0:
            for rb in range(nb // ROW_GROUP):
                srows = slice(rb * ROW_GROUP, (rb + 1) * ROW_GROUP)
                s_re, s_im = st_scr[srows, cre], st_scr[srows, cim]
                for l in range(n_rows // nb):
                    rows = slice(l * nb + rb * ROW_GROUP, l * nb + (rb + 1) * ROW_GROUP)
                    s_re, s_im = (a_re * s_re - a_im * s_im + bu_scr[rows, cre],
                                  a_re * s_im + a_im * s_re + bu_scr[rows, cim])
                    bu_scr[rows, cre] = s_re
                    bu_scr[rows, cim] = s_im
                st_scr[srows, cre] = s_re
                st_scr[srows, cim] = s_im
        else:
            ahi_re = jnp.where(hi, a_re, 0.0)
            ahi_im = jnp.where(hi, a_im, 0.0)
            ap_re = jnp.where(hi, a_re * a_re - a_im * a_im, a_re)
            ap_im = jnp.where(hi, 2.0 * a_re * a_im, a_im)
            c_re, c_im = st_scr[:, cre], st_scr[:, cim]
            for s in range(n_rows // ROW_GROUP):
                rows = slice(s * ROW_GROUP, (s + 1) * ROW_GROUP)
                v_re = bu_scr[rows, cre]
                v_im = bu_scr[rows, cim]
                r_re = pltpu.roll(v_re, half, 0)
                r_im = pltpu.roll(v_im, half, 0)
                o_re = ap_re * c_re - ap_im * c_im + (v_re + (ahi_re * r_re - ahi_im * r_im))
                o_im = ap_re * c_im + ap_im * c_re + (v_im + (ahi_re * r_im + ahi_im * r_re))
                bu_scr[rows, cre] = o_re
                bu_scr[rows, cim] = o_im
                c_re = jnp.where(hi, o_re, pltpu.roll(o_re, half, 0))
                c_im = jnp.where(hi, o_im, pltpu.roll(o_im, half, 0))
            st_scr[:, cre] = c_re
            st_scr[:, cim] = c_im

        y = y + jnp.dot(bu_scr[:, cre].astype(BF16), cd_ref[cre, :], preferred_element_type=F32)
        y = y + jnp.dot(bu_scr[:, cim].astype(BF16), cd_ref[cim, :], preferred_element_type=F32)
    y_ref[...] = jax.nn.gelu(y).astype(BF16)
    hf_ref[...] = st_scr[...]


def _ssm_core_call(xs, bd, cd, a_cat, d_skip, h0, *, nb, steps):
    n_rows, w = xs.shape
    packs, kb, ns2 = bd.shape
    ch = cd.shape[2]
    nst = h0.shape[1]
    tr = steps * nb
    return pl.pallas_call(
        functools.partial(_ssm_core_kernel, nb=nb, col_chunk=min(ns2 // 2, 256)),
        grid=(packs, n_rows // tr),
        in_specs=[pl.BlockSpec((tr, ch), lambda c, t: (t, c)),
                  pl.BlockSpec((None, kb, ns2), lambda c, t: (c, 0, 0)),
                  pl.BlockSpec((None, ns2, ch), lambda c, t: (c, 0, 0)),
                  pl.BlockSpec((None, 1, ns2), lambda c, t: (c, 0, 0)),
                  pl.BlockSpec((1, ch), lambda c, t: (0, c)),
                  pl.BlockSpec((None, nst, ns2), lambda c, t: (c, 0, 0))],
        out_specs=[pl.BlockSpec((tr, ch), lambda c, t: (t, c)),
                   pl.BlockSpec((None, nst, ns2), lambda c, t: (c, 0, 0))],
        out_shape=[jax.ShapeDtypeStruct((n_rows, w), BF16),
                   jax.ShapeDtypeStruct((packs, nst, ns2), F32)],
        scratch_shapes=[pltpu.VMEM((tr, ns2), F32), pltpu.VMEM((nst, ns2), F32)],
        compiler_params=_params("arbitrary", "arbitrary"),
        name="ssm_core",
    )(xs, bd, cd, a_cat, d_skip.reshape(1, w), h0)


def _ssm_operands(f_re, f_im, a_re, a_im, b_re, b_im, c_re, c_im):
    g, p, k = b_re.shape
    pg = SSM_PACK_GROUPS
    packs = g // pg
    bb_re = f_re[..., None] * b_re - f_im[..., None] * b_im
    bb_im = f_re[..., None] * b_im + f_im[..., None] * b_re
    eye = jnp.eye(pg, dtype=F32)

    def bdiag(m):
        m = m.reshape(packs, pg, p, k).transpose(0, 1, 3, 2)
        return (m[:, :, :, None, :] * eye[None, :, None, :, None]).reshape(packs, pg * k, pg * p)

    def cdiag(m):
        m = m.reshape(packs, pg, k, p).transpose(0, 1, 3, 2)
        return (m[:, :, :, None, :] * eye[None, :, None, :, None]).reshape(packs, pg * p, pg * k)

    bd = jnp.concatenate([bdiag(bb_re), bdiag(bb_im)], axis=2).astype(BF16)
    cd = jnp.concatenate([cdiag(c_re), cdiag(-c_im)], axis=1).astype(BF16)
    a_cat = jnp.concatenate([a_re.reshape(packs, 1, pg * p), a_im.reshape(packs, 1, pg * p)], axis=2)
    return bd, cd, a_cat


def _pack_state(s_re, s_im):
    b, g, p = s_re.shape
    packs = g // SSM_PACK_GROUPS

    def one(s):
        s = s.reshape(b, packs, SSM_PACK_GROUPS * p).transpose(1, 0, 2)
        return s if b % ROW_GROUP == 0 else jnp.tile(s, (1, ROW_GROUP // b, 1))

    return jnp.concatenate([one(s_re), one(s_im)], axis=2)


def _unpack_state(hf, b, g, p):
    ns = hf.shape[2] // 2

    def one(s):
        return s[:, :b].transpose(1, 0, 2).reshape(b, g, p)

    return one(hf[:, :, :ns]), one(hf[:, :, ns:])


PEER_CAND_ROWS = PEER_TOPK + (PEER_TOPK // 2 - 1) * (PEER_TOPK // 2) + PEER_TOPK // 2


def _sorted_top(s, kk):
    n, lanes = s.shape
    groups = n // ROW_GROUP
    assert n % ROW_GROUP == 0 and groups <= kk and kk & (kk - 1) == 0
    v = [s[i * ROW_GROUP:(i + 1) * ROW_GROUP] for i in range(groups)]
    v += [jnp.full((ROW_GROUP, lanes), -jnp.inf, F32)] * (kk - groups)

    def exchange(i, l, descending):
        big, small = jnp.maximum(v[i], v[l]), jnp.minimum(v[i], v[l])
        v[i], v[l] = (big, small) if descending else (small, big)

    k = 2
    while k <= kk:
        j = k // 2
        while j >= 1:
            for i in range(kk):
                if i ^ j > i:
                    exchange(i, i ^ j, (i & k) == 0)
            j //= 2
        k *= 2
    shift = ROW_GROUP // 2
    while shift >= 1:
        other = [pltpu.roll(v[kk - 1 - i], shift, 0) for i in range(kk)]
        v = [jnp.maximum(v[i], other[i]) for i in range(kk)]
        j = kk // 2
        while j >= 1:
            for i in range(kk):
                if i ^ j > i:
                    exchange(i, i ^ j, True)
            j //= 2
        shift //= 2
    return v


def _peer_select_kernel(ht_ref, wq_ref, keys_ref, c0_ref, e0_ref, a1_ref, e1_ref, top_scr, cand_scr,
                        cs_scr, s_scr, *, lane_chunk):
    hps, _, _, dk = keys_ref.shape
    kk, hk = PEER_TOPK, PEER_TOPK // 2
    q = jnp.dot(wq_ref[...], ht_ref[...], preferred_element_type=F32)
    tt = ht_ref.shape[1]
    for hh in range(hps):
        top, cand, cs = top_scr.at[hh], cand_scr.at[hh], cs_scr.at[hh]
        for z in range(2):
            qz = q[(2 * hh + z) * dk:(2 * hh + z + 1) * dk]
            s_scr[hh, z] = jnp.dot(keys_ref[hh, z], qz, preferred_element_type=F32,
                                   precision=lax.Precision.HIGHEST)
        for lc in range(tt // lane_chunk):
            cols = slice(lc * lane_chunk, (lc + 1) * lane_chunk)
            for z in range(2):
                best_z = _sorted_top(s_scr[hh, z, :, cols], kk)
                for k in range(kk):
                    top[z, k:k + 1, cols] = best_z[k][0:1, :]
            cand[0:kk, cols] = top[0, 0:1, cols] + top[1, :, cols]
            for i in range(1, hk):
                cand[kk + (i - 1) * hk:kk + i * hk, cols] = top[0, i:i + 1, cols] + top[1, 0:hk, cols]
            cand[kk + (hk - 1) * hk:, cols] = top[0, hk:kk, cols] + top[1, 0:1, cols]
            pair_top = _sorted_top(cand[:, cols], kk)
            best = top[0, 0:1, cols] + top[1, 0:1, cols]
            zsum = jnp.zeros_like(best)
            for k in range(kk):
                zsum = zsum + jnp.exp(pair_top[k][0:1, :] - best)
            thr = pair_top[kk - 1][0:1, :]
            a_sorted = top[0, :, cols]
            c_sorted = jnp.full(a_sorted.shape, jnp.inf, F32)
            for j in range(kk):
                bj = top[1, j:j + 1, cols]
                c_sorted = jnp.where(a_sorted + bj >= thr, bj, c_sorted)
            cs[:, cols] = c_sorted
            s0 = s_scr[hh, 0, :, cols]
            s1 = s_scr[hh, 1, :, cols]
            c0 = jnp.full(s0.shape, jnp.inf, F32)
            for i in range(kk):
                c0 = jnp.where(s0 == top[0, i:i + 1, cols], cs[i:i + 1, cols], c0)
            c0_ref[hh, :, cols] = c0
            e0_ref[hh, :, cols] = jnp.exp(s0 - top[0, 0:1, cols]) / zsum
            a1_ref[hh, :, cols] = s1
            e1_ref[hh, :, cols] = jnp.exp(s1 - top[1, 0:1, cols])


def _peer_select_call(ht, wq_t, keys, layer):
    d, t = ht.shape
    heads, _, n_keys, dk = keys.shape[1:]
    tt = _tile(t, 512)
    hps = 2 if heads % 2 == 0 else 1
    sc_spec = pl.BlockSpec((hps, n_keys, tt), lambda i, h: (h, 0, i))
    sc_shape = jax.ShapeDtypeStruct((heads, n_keys, t), F32)
    return pl.pallas_call(
        functools.partial(_peer_select_kernel, lane_chunk=min(tt, 128)),
        grid=(t // tt, heads // hps),
        in_specs=[pl.BlockSpec((d, tt), lambda i, h: (0, i)),
                  pl.BlockSpec((None, hps * 2 * dk, d), lambda i, h: (layer, h, 0)),
                  pl.BlockSpec((None, hps, 2, n_keys, dk), lambda i, h: (layer, h, 0, 0, 0))],
        out_specs=[sc_spec] * 4,
        out_shape=[sc_shape] * 4,
        scratch_shapes=[pltpu.VMEM((hps, 2, PEER_TOPK, tt), F32),
                        pltpu.VMEM((hps, PEER_CAND_ROWS, tt), F32),
                        pltpu.VMEM((hps, PEER_TOPK, tt), F32),
                        pltpu.VMEM((hps, 2, n_keys, tt), F32)],
        compiler_params=_params("arbitrary", "arbitrary"),
        name="peer_select",
    )(ht, wq_t, keys)


def _peer_dense_kernel(ht_ref, u_ref, vt_ref, c0t_ref, e0t_ref, a1_ref, e1_ref, o_ref, act_scr, p_scr,
                       *, lane_chunk):
    nsub, heads, tt = c0t_ref.shape
    n2 = a1_ref.shape[1]
    j = pl.program_id(1)
    slot = j % 2

    @pl.when(j == 0)
    def _():
        o_ref[...] = jnp.zeros_like(o_ref)
        act_scr[1] = jnp.zeros(act_scr.shape[1:], F32)

    for r in range(nsub):
        rows = slice(r * n2, (r + 1) * n2)
        for lc in range(tt // lane_chunk):
            cols = slice(lc * lane_chunk, (lc + 1) * lane_chunk)
            w = jnp.zeros((n2, lane_chunk), F32)
            for h in range(heads):
                hit = a1_ref[h, :, cols] >= c0t_ref[r, h:h + 1, cols]
                w = w + jnp.where(hit, e0t_ref[r, h:h + 1, cols] * e1_ref[h, :, cols], 0.0)
            p_scr[rows, cols] = (w * act_scr[1 - slot, rows, cols]).astype(BF16)
    act_scr[slot] = jax.nn.gelu(jnp.dot(u_ref[...], ht_ref[...], preferred_element_type=F32))
    o_ref[...] += jnp.dot(vt_ref[...], p_scr[...], preferred_element_type=F32)


def _peer_dense_call(ht, u_bf, vt_bf, c0t, e0t, a1, e1, layer, *, ec=512):
    d, t = ht.shape
    n_exp = u_bf.shape[1]
    heads, n2, _ = a1.shape
    tt = _tile(t, 512)
    ec = _tile(n_exp, ec)
    nsub = ec // n2
    n_chunks = n_exp // ec
    cur = lambda j: jnp.minimum(j, n_chunks - 1)
    prev = lambda j: jnp.maximum(j - 1, 0)
    row_spec = pl.BlockSpec((nsub, heads, tt), lambda i, j: (prev(j), 0, i))
    once = pl.Buffered(1)
    full_spec = pl.BlockSpec((heads, n2, tt), lambda i, j: (0, 0, i), pipeline_mode=once)
    return pl.pallas_call(
        functools.partial(_peer_dense_kernel, lane_chunk=min(tt, 256)),
        grid=(t // tt, n_chunks + 1),
        in_specs=[pl.BlockSpec((d, tt), lambda i, j: (0, i), pipeline_mode=once),
                  pl.BlockSpec((None, ec, d), lambda i, j: (layer, cur(j), 0)),
                  pl.BlockSpec((None, d, ec), lambda i, j: (layer, 0, prev(j))),
                  row_spec, row_spec, full_spec, full_spec],
        out_specs=pl.BlockSpec((d, tt), lambda i, j: (0, i)),
        out_shape=jax.ShapeDtypeStruct((d, t), F32),
        scratch_shapes=[pltpu.VMEM((2, ec, tt), F32), pltpu.VMEM((ec, tt), BF16)],
        compiler_params=_params("arbitrary", "arbitrary"),
        name="peer_dense",
    )(ht, u_bf, vt_bf, c0t, e0t, a1, e1)


def _peer_ffn_t(ht, wq_t, keys, u_bf, vt_bf, layer):
    c0, e0, a1, e1 = _peer_select_call(ht, wq_t, keys, layer)
    return _peer_dense_call(ht, u_bf, vt_bf, c0.transpose(1, 0, 2), e0.transpose(1, 0, 2), a1, e1, layer)


def kernel(x_prompt, x_sample, c_prompt, c_sample, state_ssm_re, state_ssm_im, ada_w, ada_b, norm_mix_g, norm_ffn_g, final_g, gm_w_in, gm_b_in, gm_v_g, gm_w_s, gm_b_s, gm_w_out, ssm_w_in, ssm_lam_re, ssm_lam_im, ssm_log_dt, ssm_b_re, ssm_b_im, ssm_c_re, ssm_c_im, ssm_d, ssm_w_out, peer_w_q, peer_keys, peer_u, peer_v):
    bp, seq, d = x_prompt.shape
    bs, dseq, _ = x_sample.shape
    depth = ada_w.shape[0]
    tp, ts = bp * seq, bs * dseq
    chunk = gm_w_s.shape[2]
    hw = gm_w_in.shape[2] // 2
    n_groups, n_state = ssm_lam_re.shape[1:]
    w_ssm = ssm_w_in.shape[2]
    assert dseq == ROW_GROUP and seq % chunk == 0 and ts % chunk == 0
    assert bp * 2 == ROW_GROUP and bs % ROW_GROUP == 0
    groups = (bp, seq, bs)

    x = (x_prompt.reshape(tp, d), x_sample.reshape(ts, d))

    m_real = bp + bs
    m_pad = -(-m_real // ROW_GROUP) * ROW_GROUP
    c_all = jnp.pad(jnp.concatenate([c_prompt, c_sample], axis=0), ((0, m_pad - m_real), (0, 0)))
    mods = _ada_call(c_all, ada_w, ada_b)

    gm_w_in_bf, gm_w_out_bf = gm_w_in.astype(BF16), gm_w_out.astype(BF16)
    ssm_w_in_bf, ssm_w_out_bf = ssm_w_in.astype(BF16), ssm_w_out.astype(BF16)
    wq_t = peer_w_q.transpose(0, 2, 1).astype(BF16)
    u_bf = peer_u.astype(BF16)
    vt_bf = peer_v.transpose(0, 2, 1).astype(BF16)

    tril = jnp.tril(jnp.ones((chunk, chunk), F32))
    rep = chunk // dseq
    tril_s = jnp.tril(jnp.ones((dseq, dseq), F32))

    def spatial_operands(ia):
        w_p = gm_w_s[ia] * tril
        w_s = jnp.einsum("ab,gts->gatbs", jnp.eye(rep, dtype=F32),
                         gm_w_s[ia][:, :dseq, :dseq] * tril_s).reshape(-1, chunk, chunk)
        b_p = gm_b_s[ia][:, :, None]
        b_s = jnp.tile(gm_b_s[ia][:, :dseq], (1, rep))[:, :, None]
        return jnp.stack([w_p, w_s]).astype(BF16), jnp.stack([b_p, b_s])

    new_v, st_p, st_s = [], [], []
    peer_t = None
    me_prev = None
    ia = ib = 0
    for i in range(depth):
        me = _group_rows(mods[i], groups)
        if i == 0:
            (h,) = _resnorm_call(x, norm_mix_g[i], groups, me, sc_col=1, sh_col=0)
        else:
            x, h = _resnorm_call(x, norm_mix_g[i], groups, me, delta=peer_t, delta_t=True,
                                 me_gate=me_prev, gate_col=5, sc_col=1, sh_col=0)
        if i % 2 == 0:
            u = _mm_call(h, gm_w_in_bf, ia, bias=gm_b_in, act="gelu", out_dtype=BF16, n_out=hw)
            v_raw = _mm_call(h, gm_w_in_bf, ia, bias=gm_b_in, act="gelu", col_start=hw, n_out=hw)
            wmix, bmix = spatial_operands(ia)
            gated, v_s = _gmlp_gate_call(u, v_raw, gm_v_g[ia], wmix, bmix, tp // chunk)
            new_v.append(v_s.reshape(bs, dseq, hw))
            mix = _mm_call(gated, gm_w_out_bf, ia, tm=768, tn=512)
            ia += 1
        else:
            a_re, a_im, f_re, f_im = _ssm_disc_call(ssm_lam_re[ib], ssm_lam_im[ib], ssm_log_dt[ib])
            bd, cd, a_cat = _ssm_operands(f_re, f_im, a_re, a_im, ssm_b_re[ib], ssm_b_im[ib],
                                          ssm_c_re[ib], ssm_c_im[ib])
            xs_p = _mm_call(h, ssm_w_in_bf, ib, m_rows=tp, seq_major=(bp, seq)).reshape(tp, w_ssm)
            zero = jnp.zeros((bp, n_groups, n_state), F32)
            y_p, hf_p = _ssm_core_call(xs_p, bd, cd, a_cat, ssm_d[ib], _pack_state(zero, zero),
                                       nb=bp, steps=_tile(seq, 256))
            mix_p = _glu_call(y_p.reshape(seq, bp * w_ssm), ssm_w_out_bf, ib, seq_major=(bp, seq))
            xs_s = _mm_call(h, ssm_w_in_bf, ib, row_start=tp, m_rows=ts)
            xs_s = xs_s.reshape(bs, dseq, w_ssm).transpose(1, 0, 2).reshape(ts, w_ssm)
            y_s, hf_s = _ssm_core_call(xs_s, bd, cd, a_cat, ssm_d[ib],
                                       _pack_state(state_ssm_re[ib], state_ssm_im[ib]), nb=bs, steps=dseq)
            y_s = y_s.reshape(dseq, bs, w_ssm).transpose(1, 0, 2).reshape(ts, w_ssm)
            mix = (mix_p, _glu_call(y_s, ssm_w_out_bf, ib))
            st_p.append(_unpack_state(hf_p, bp, n_groups, n_state))
            st_s.append(_unpack_state(hf_s, bs, n_groups, n_state))
            ib += 1
        x, ht = _resnorm_call(x, norm_ffn_g[i], groups, me, delta=mix, me_gate=me, gate_col=2, sc_col=4,
                              sh_col=3, emit_h=False, emit_ht=True)
        peer_t = _peer_ffn_t(ht, wq_t, peer_keys, u_bf, vt_bf, i)
        me_prev = me
    y_p, y_s = _resnorm_call(x, final_g, groups, delta=peer_t, delta_t=True, me_gate=me_prev, gate_col=5,
                             final=True)

    y_prompt = y_p.reshape(bp, seq, d)
    y_sample = y_s.reshape(bs, dseq, d)
    return (y_prompt, y_sample, jnp.stack(new_v),
            jnp.stack([s[0] for s in st_p]), jnp.stack([s[1] for s in st_p]),
            jnp.stack([s[0] for s in st_s]), jnp.stack([s[1] for s in st_s]))
```

```python
import functools
import math

import jax
import jax.numpy as jnp
from jax import lax
from jax.experimental import pallas as pl
from jax.experimental.pallas import tpu as pltpu

EPS = 1e-6
PEER_TOPK = 16
ROW_GROUP = 8
SSM_PACK_GROUPS = 16
VMEM_LIMIT_BYTES = 56 * 1024 * 1024
F32 = jnp.float32
BF16 = jnp.bfloat16


def _params(*sem):
    return pltpu.CompilerParams(dimension_semantics=sem, vmem_limit_bytes=VMEM_LIMIT_BYTES)


def _tile(n, pref, align=128):
    if n <= pref:
        return n
    for cand in range(pref - pref % align, 0, -align):
        if n % cand == 0:
            return cand
    raise ValueError((n, pref, align))


def _ada_kernel(c_ref, w_ref, b_ref, o_ref):
    c = c_ref[...]
    act = (c * (1.0 / (1.0 + jnp.exp(-c)))).astype(BF16)
    o_ref[...] = jnp.dot(act, w_ref[...].astype(BF16), preferred_element_type=F32) + b_ref[...]


def _ada_call(c_all, ada_w, ada_b):
    depth, d, n = ada_w.shape
    m = c_all.shape[0]
    tn = _tile(n, 512)
    return pl.pallas_call(
        _ada_kernel,
        grid=(depth, n // tn),
        in_specs=[pl.BlockSpec((m, d), lambda l, j: (0, 0)),
                  pl.BlockSpec((None, d, tn), lambda l, j: (l, 0, j)),
                  pl.BlockSpec((None, 1, tn), lambda l, j: (l, 0, j))],
        out_specs=pl.BlockSpec((None, m, tn), lambda l, j: (l, 0, j)),
        out_shape=jax.ShapeDtypeStruct((depth, m, n), F32),
        compiler_params=_params("arbitrary", "arbitrary"),
        name="ada_mod",
    )(c_all, ada_w, ada_b.reshape(depth, 1, n))


def _resnorm_kernel(*refs, x_split, has_delta, delta_t, delta_split, emit_h, emit_ht, final,
                    n_prompt_tiles):
    it = iter(refs)
    x_ref = next(it)
    xs_ref = next(it) if x_split else None
    in_prompt = pl.program_id(0) < n_prompt_tiles
    if has_delta:
        delta_ref = next(it)
        delta_s_ref = next(it) if delta_split else None
        gate_ref = next(it)
    g_ref = next(it)
    if not final:
        sc_ref = next(it)
        sh_ref = next(it)
    tm, d = x_ref.shape
    r = tm // ROW_GROUP
    x = x_ref[...]
    if x_split:
        x = jnp.where(in_prompt, x, xs_ref[...])
    x3 = x.reshape(r, ROW_GROUP, d)
    if has_delta:
        delta = delta_ref[...]
        if delta_split:
            delta = jnp.where(in_prompt, delta, delta_s_ref[...])
        if delta_t:
            delta = delta.T
        x3 = x3 + gate_ref[...][:, None, :] * delta.reshape(r, ROW_GROUP, d)
    var = jnp.mean(x3 * x3, axis=-1, keepdims=True)
    nrm = x3 * lax.rsqrt(var + EPS) * g_ref[...]
    if final:
        yp_ref, ys_ref = next(it), next(it)

        @pl.when(in_prompt)
        def _():
            yp_ref[...] = nrm.reshape(tm, d)

        @pl.when(jnp.logical_not(in_prompt))
        def _():
            ys_ref[...] = nrm.reshape(tm, d)

        return
    h = (nrm * (1.0 + sc_ref[...][:, None, :]) + sh_ref[...][:, None, :]).reshape(tm, d)
    if has_delta:
        next(it)[...] = x3.reshape(tm, d)
    if emit_h:
        next(it)[...] = h.astype(BF16)
    if emit_ht:
        next(it)[...] = h.T.astype(BF16)


def _resnorm_tile(groups):
    bp, seq, bs = groups
    return _tile(math.gcd(seq, bs * ROW_GROUP), 256)


def _group_rows(mod, groups):
    bp, _, bs = groups
    r = _resnorm_tile(groups) // ROW_GROUP
    return jnp.concatenate([jnp.repeat(mod[:bp], r, axis=0), mod[bp:bp + bs]], axis=0)


def _resnorm_call(x, gain, groups, me=None, *, delta=None, delta_t=False, me_gate=None, gate_col=None,
                  sc_col=None, sh_col=None, emit_h=True, emit_ht=False, final=False):
    x_split = isinstance(x, tuple)
    d = x[0].shape[1] if x_split else x.shape[1]
    bp, seq, bs = groups
    t = bp * seq + bs * ROW_GROUP
    tm = _resnorm_tile(groups)
    r = tm // ROW_GROUP
    tps = seq // tm
    n_prompt_tiles = bp * tps
    has_delta = delta is not None
    row = pl.BlockSpec((tm, d), lambda i: (i, 0))
    col = pl.BlockSpec((d, tm), lambda i: (0, i))

    def me_spec(k):
        return pl.BlockSpec(
            (r, d), lambda i, k=k: (jnp.where(i < n_prompt_tiles, i // tps, bp + i - n_prompt_tiles), k))

    row_p = pl.BlockSpec((tm, d), lambda i: (jnp.minimum(i, n_prompt_tiles - 1), 0))
    row_s = pl.BlockSpec((tm, d), lambda i: (jnp.maximum(i - n_prompt_tiles, 0), 0))
    delta_split = isinstance(delta, tuple)
    in_specs, args = ([row_p, row_s], list(x)) if x_split else ([row], [x])
    if has_delta:
        if delta_split:
            assert not delta_t
            in_specs += [row_p, row_s]
            args += list(delta)
        else:
            in_specs.append(col if delta_t else row)
            args.append(delta)
        in_specs.append(me_spec(gate_col))
        args.append(me_gate)
    in_specs.append(pl.BlockSpec((1, d), lambda i: (0, 0)))
    args.append(gain.reshape(1, d))
    if not final:
        in_specs += [me_spec(sc_col), me_spec(sh_col)]
        args += [me, me]
    out_specs, out_shape = [], []
    if final:
        out_specs += [row_p, row_s]
        out_shape += [jax.ShapeDtypeStruct((n_prompt_tiles * tm, d), F32),
                      jax.ShapeDtypeStruct((t - n_prompt_tiles * tm, d), F32)]
    elif has_delta:
        out_specs.append(row)
        out_shape.append(jax.ShapeDtypeStruct((t, d), F32))
    if not final and emit_h:
        out_specs.append(row)
        out_shape.append(jax.ShapeDtypeStruct((t, d), BF16))
    if not final and emit_ht:
        out_specs.append(col)
        out_shape.append(jax.ShapeDtypeStruct((d, t), BF16))
    return pl.pallas_call(
        functools.partial(_resnorm_kernel, x_split=x_split, has_delta=has_delta, delta_t=delta_t,
                          delta_split=delta_split,
                          emit_h=emit_h, emit_ht=emit_ht, final=final, n_prompt_tiles=n_prompt_tiles),
        grid=(t // tm,),
        in_specs=in_specs, out_specs=out_specs, out_shape=out_shape,
        compiler_params=_params("arbitrary"),
        name="resnorm",
    )(*args)


def _mm_kernel(a_ref, b_ref, *rest, has_bias, act, row_block=512):
    o_ref = rest[-1]
    tm = a_ref.shape[0]
    rb = row_block if (has_bias or act) and tm % row_block == 0 else tm
    for r0 in range(0, tm, rb):
        acc = jnp.dot(a_ref[r0:r0 + rb, :], b_ref[...], preferred_element_type=F32)
        if has_bias:
            acc = acc + rest[0][...]
        if act == "gelu":
            acc = jax.nn.gelu(acc)
        o_ref[r0:r0 + rb, :] = acc.astype(o_ref.dtype)


def _mm_call(a, b, layer, *, bias=None, act=None, out_dtype=F32, col_start=0, n_out=None,
             row_start=0, m_rows=None, seq_major=None, tm=1536, tn=512):
    k = a.shape[1]
    m = a.shape[0] if m_rows is None else m_rows
    n_out = b.shape[2] if n_out is None else n_out
    tm = _tile(m if seq_major is None else seq_major[1], tm)
    tn = _tile(n_out, tn)
    assert col_start % tn == 0 and row_start % tm == 0
    off, roff, nj = col_start // tn, row_start // tm, n_out // tn
    in_specs = [pl.BlockSpec((tm, k), lambda i, j: (i + roff, 0)),
                pl.BlockSpec((None, k, tn), lambda i, j: (layer, 0, j + off))]
    args = [a, b]
    if bias is not None:
        in_specs.append(pl.BlockSpec((None, 1, tn), lambda i, j: (layer, 0, j + off)))
        args.append(bias.reshape(bias.shape[0], 1, -1))
    if seq_major is None:
        out_spec = pl.BlockSpec((tm, tn), lambda i, j: (i, j))
        out_shape = jax.ShapeDtypeStruct((m, n_out), out_dtype)
    else:
        n_seq, seq_len = seq_major
        tps = seq_len // tm
        out_spec = pl.BlockSpec((tm, tn), lambda i, j: (i % tps, (i // tps) * nj + j))
        out_shape = jax.ShapeDtypeStruct((seq_len, n_seq * n_out), out_dtype)
    return pl.pallas_call(
        functools.partial(_mm_kernel, has_bias=bias is not None, act=act),
        grid=(m // tm, nj),
        in_specs=in_specs,
        out_specs=out_spec,
        out_shape=out_shape,
        compiler_params=_params("arbitrary", "arbitrary"),
        name="matmul",
    )(*args)


def _glu_kernel(a_ref, ba_ref, bg_ref, o_ref):
    a = a_ref[...]
    lin = jnp.dot(a, ba_ref[...], preferred_element_type=F32)
    gat = jnp.dot(a, bg_ref[...], preferred_element_type=F32)
    o_ref[...] = lin * (1.0 / (1.0 + jnp.exp(-gat)))


def _glu_call(a, b, layer, *, seq_major=None, tm=1024, tn=256):
    k = b.shape[1]
    n = b.shape[2] // 2
    if seq_major is None:
        m = a.shape[0]
        tm = _tile(m, tm)
        a_spec = pl.BlockSpec((tm, k), lambda i, j: (i, 0))
    else:
        n_seq, seq_len = seq_major
        m = n_seq * seq_len
        tm = _tile(seq_len, tm)
        tps = seq_len // tm
        a_spec = pl.BlockSpec((tm, k), lambda i, j: (i % tps, i // tps))
    tn = _tile(n, tn)
    off = n // tn
    return pl.pallas_call(
        _glu_kernel,
        grid=(m // tm, n // tn),
        in_specs=[a_spec,
                  pl.BlockSpec((None, k, tn), lambda i, j: (layer, 0, j)),
                  pl.BlockSpec((None, k, tn), lambda i, j: (layer, 0, j + off))],
        out_specs=pl.BlockSpec((tm, tn), lambda i, j: (i, j)),
        out_shape=jax.ShapeDtypeStruct((m, n), F32),
        compiler_params=_params("arbitrary", "arbitrary"),
        name="glu_matmul",
    )(a, b, b)


def _gmlp_gate_kernel(u_ref, v_ref, vg_ref, w_ref, b_ref, o_ref, vs_ref, *, groups, n_prompt_tiles):
    v = v_ref[...]
    var = jnp.mean(v * v, axis=-1, keepdims=True)
    vn = v * lax.rsqrt(var + EPS) * vg_ref[...]

    @pl.when(pl.program_id(0) >= n_prompt_tiles)
    def _():
        vs_ref[...] = vn

    gdim = v.shape[1] // groups
    for g in range(groups):
        sl = slice(g * gdim, (g + 1) * gdim)
        s = jnp.dot(w_ref[g], vn[:, sl].astype(BF16), preferred_element_type=F32) + b_ref[g]
        o_ref[:, sl] = (u_ref[:, sl].astype(F32) * s).astype(BF16)


def _gmlp_gate_call(u, v_raw, v_g, wmix, bmix, n_prompt_tiles):
    t, hw = u.shape
    _, groups, blk, _ = wmix.shape
    n_tiles = t // blk
    kind = lambda i: jnp.where(i >= n_prompt_tiles, 1, 0)
    return pl.pallas_call(
        functools.partial(_gmlp_gate_kernel, groups=groups, n_prompt_tiles=n_prompt_tiles),
        grid=(n_tiles,),
        in_specs=[pl.BlockSpec((blk, hw), lambda i: (i, 0)),
                  pl.BlockSpec((blk, hw), lambda i: (i, 0)),
                  pl.BlockSpec((1, hw), lambda i: (0, 0)),
                  pl.BlockSpec((None, groups, blk, blk), lambda i: (kind(i), 0, 0, 0)),
                  pl.BlockSpec((None, groups, blk, 1), lambda i: (kind(i), 0, 0, 0))],
        out_specs=[pl.BlockSpec((blk, hw), lambda i: (i, 0)),
                   pl.BlockSpec((blk, hw), lambda i: (jnp.maximum(i - n_prompt_tiles, 0), 0))],
        out_shape=[jax.ShapeDtypeStruct((t, hw), BF16),
                   jax.ShapeDtypeStruct(((n_tiles - n_prompt_tiles) * blk, hw), F32)],
        compiler_params=_params("arbitrary"),
        name="gmlp_gate",
    )(u, v_raw, v_g.reshape(1, hw), wmix, bmix)


def _ssm_disc_kernel(lre_ref, lim_ref, ldt_ref, are_ref, aim_ref, fre_ref, fim_ref):
    lre = lre_ref[...]
    lim = lim_ref[...]
    dt = jnp.exp(ldt_ref[...])
    mag = jnp.exp(lre * dt)
    ang = lim * dt
    a_re = mag * jnp.cos(ang)
    a_im = mag * jnp.sin(ang)
    n_re = a_re - 1
    den = lre * lre + lim * lim
    are_ref[...] = a_re
    aim_ref[...] = a_im
    fre_ref[...] = (n_re * lre + a_im * lim) / den
    fim_ref[...] = (a_im * lre - n_re * lim) / den


def _ssm_disc_call(lam_re, lam_im, log_dt):
    g, p = lam_re.shape
    shp = jax.ShapeDtypeStruct((g, p), F32)
    return pl.pallas_call(_ssm_disc_kernel, out_shape=[shp] * 4, name="ssm_discretize")(
        lam_re, lam_im, log_dt.reshape(g, 1))


def _ssm_core_kernel(xs_ref, bd_ref, cd_ref, a_ref, d_ref, h0_ref, y_ref, hf_ref, bu_scr, st_scr,
                     *, nb, col_chunk):
    n_rows, ch = xs_ref.shape
    ns = a_ref.shape[1] // 2
    half = ROW_GROUP // 2

    @pl.when(pl.program_id(1) == 0)
    def _():
        st_scr[...] = h0_ref[...]

    xs = xs_ref[...]
    xs_bf = xs.astype(BF16)
    y = d_ref[...] * xs
    if nb % ROW_GROUP != 0:
        assert nb == half
        hi = lax.broadcasted_iota(jnp.int32, (ROW_GROUP, col_chunk), 0) >= half

    for cc in range(ns // col_chunk):
        cre = slice(cc * col_chunk, (cc + 1) * col_chunk)
        cim = slice(ns + cc * col_chunk, ns + (cc + 1) * col_chunk)
        a_re = jnp.broadcast_to(a_ref[:, cre], (ROW_GROUP, col_chunk))
        a_im = jnp.broadcast_to(a_ref[:, cim], (ROW_GROUP, col_chunk))
        bu_scr[:, cre] = jnp.dot(xs_bf, bd_ref[:, cre], preferred_element_type=F32)
        bu_scr[:, cim] = jnp.dot(xs_bf, bd_ref[:, cim], preferred_element_type=F32)

        if nb % ROW_GROUP == 0:
            for rb in range(nb // ROW_GROUP):
                srows = slice(rb * ROW_GROUP, (rb + 1) * ROW_GROUP)
                s_re, s_im = st_scr[srows, cre], st_scr[srows, cim]
                for l in range(n_rows // nb):
                    rows = slice(l * nb + rb * ROW_GROUP, l * nb + (rb + 1) * ROW_GROUP)
                    s_re, s_im = (a_re * s_re - a_im * s_im + bu_scr[rows, cre],
                                  a_re * s_im + a_im * s_re + bu_scr[rows, cim])
                    bu_scr[rows, cre] = s_re
                    bu_scr[rows, cim] = s_im
                st_scr[srows, cre] = s_re
                st_scr[srows, cim] = s_im
        else:
            ahi_re = jnp.where(hi, a_re, 0.0)
            ahi_im = jnp.where(hi, a_im, 0.0)
            ap_re = jnp.where(hi, a_re * a_re - a_im * a_im, a_re)
            ap_im = jnp.where(hi, 2.0 * a_re * a_im, a_im)
            c_re, c_im = st_scr[:, cre], st_scr[:, cim]
            for s in range(n_rows // ROW_GROUP):
                rows = slice(s * ROW_GROUP, (s + 1) * ROW_GROUP)
                v_re = bu_scr[rows, cre]
                v_im = bu_scr[rows, cim]
                r_re = pltpu.roll(v_re, half, 0)
                r_im = pltpu.roll(v_im, half, 0)
                o_re = ap_re * c_re - ap_im * c_im + (v_re + (ahi_re * r_re - ahi_im * r_im))
                o_im = ap_re * c_im + ap_im * c_re + (v_im + (ahi_re * r_im + ahi_im * r_re))
                bu_scr[rows, cre] = o_re
                bu_scr[rows, cim] = o_im
                c_re = jnp.where(hi, o_re, pltpu.roll(o_re, half, 0))
                c_im = jnp.where(hi, o_im, pltpu.roll(o_im, half, 0))
            st_scr[:, cre] = c_re
            st_scr[:, cim] = c_im

        y = y + jnp.dot(bu_scr[:, cre].astype(BF16), cd_ref[cre, :], preferred_element_type=F32)
        y = y + jnp.dot(bu_scr[:, cim].astype(BF16), cd_ref[cim, :], preferred_element_type=F32)
    y_ref[...] = jax.nn.gelu(y).astype(BF16)
    hf_ref[...] = st_scr[...]


def _ssm_core_call(xs, bd, cd, a_cat, d_skip, h0, *, nb, steps):
    n_rows, w = xs.shape
    packs, kb, ns2 = bd.shape
    ch = cd.shape[2]
    nst = h0.shape[1]
    tr = steps * nb
    return pl.pallas_call(
        functools.partial(_ssm_core_kernel, nb=nb, col_chunk=min(ns2 // 2, 256)),
        grid=(packs, n_rows // tr),
        in_specs=[pl.BlockSpec((tr, ch), lambda c, t: (t, c)),
                  pl.BlockSpec((None, kb, ns2), lambda c, t: (c, 0, 0)),
                  pl.BlockSpec((None, ns2, ch), lambda c, t: (c, 0, 0)),
                  pl.BlockSpec((None, 1, ns2), lambda c, t: (c, 0, 0)),
                  pl.BlockSpec((1, ch), lambda c, t: (0, c)),
                  pl.BlockSpec((None, nst, ns2), lambda c, t: (c, 0, 0))],
        out_specs=[pl.BlockSpec((tr, ch), lambda c, t: (t, c)),
                   pl.BlockSpec((None, nst, ns2), lambda c, t: (c, 0, 0))],
        out_shape=[jax.ShapeDtypeStruct((n_rows, w), BF16),
                   jax.ShapeDtypeStruct((packs, nst, ns2), F32)],
        scratch_shapes=[pltpu.VMEM((tr, ns2), F32), pltpu.VMEM((nst, ns2), F32)],
        compiler_params=_params("arbitrary", "arbitrary"),
        name="ssm_core",
    )(xs, bd, cd, a_cat, d_skip.reshape(1, w), h0)


def _ssm_operands(f_re, f_im, a_re, a_im, b_re, b_im, c_re, c_im):
    g, p, k = b_re.shape
    pg = SSM_PACK_GROUPS
    packs = g // pg
    bb_re = f_re[..., None] * b_re - f_im[..., None] * b_im
    bb_im = f_re[..., None] * b_im + f_im[..., None] * b_re
    eye = jnp.eye(pg, dtype=F32)

    def bdiag(m):
        m = m.reshape(packs, pg, p, k).transpose(0, 1, 3, 2)
        return (m[:, :, :, None, :] * eye[None, :, None, :, None]).reshape(packs, pg * k, pg * p)

    def cdiag(m):
        m = m.reshape(packs, pg, k, p).transpose(0, 1, 3, 2)
        return (m[:, :, :, None, :] * eye[None, :, None, :, None]).reshape(packs, pg * p, pg * k)

    bd = jnp.concatenate([bdiag(bb_re), bdiag(bb_im)], axis=2).astype(BF16)
    cd = jnp.concatenate([cdiag(c_re), cdiag(-c_im)], axis=1).astype(BF16)
    a_cat = jnp.concatenate([a_re.reshape(packs, 1, pg * p), a_im.reshape(packs, 1, pg * p)], axis=2)
    return bd, cd, a_cat


def _pack_state(s_re, s_im):
    b, g, p = s_re.shape
    packs = g // SSM_PACK_GROUPS

    def one(s):
        s = s.reshape(b, packs, SSM_PACK_GROUPS * p).transpose(1, 0, 2)
        return s if b % ROW_GROUP == 0 else jnp.tile(s, (1, ROW_GROUP // b, 1))

    return jnp.concatenate([one(s_re), one(s_im)], axis=2)


def _unpack_state(hf, b, g, p):
    ns = hf.shape[2] // 2

    def one(s):
        return s[:, :b].transpose(1, 0, 2).reshape(b, g, p)

    return one(hf[:, :, :ns]), one(hf[:, :, ns:])


PEER_CAND_ROWS = PEER_TOPK + (PEER_TOPK // 2 - 1) * (PEER_TOPK // 2) + PEER_TOPK // 2


def _sorted_top(s, kk):
    n, lanes = s.shape
    groups = n // ROW_GROUP
    assert n % ROW_GROUP == 0 and groups <= kk and kk & (kk - 1) == 0
    v = [s[i * ROW_GROUP:(i + 1) * ROW_GROUP] for i in range(groups)]
    v += [jnp.full((ROW_GROUP, lanes), -jnp.inf, F32)] * (kk - groups)

    def exchange(i, l, descending):
        big, small = jnp.maximum(v[i], v[l]), jnp.minimum(v[i], v[l])
        v[i], v[l] = (big, small) if descending else (small, big)

    k = 2
    while k <= kk:
        j = k // 2
        while j >= 1:
            for i in range(kk):
                if i ^ j > i:
                    exchange(i, i ^ j, (i & k) == 0)
            j //= 2
        k *= 2
    shift = ROW_GROUP // 2
    while shift >= 1:
        other = [pltpu.roll(v[kk - 1 - i], shift, 0) for i in range(kk)]
        v = [jnp.maximum(v[i], other[i]) for i in range(kk)]
        j = kk // 2
        while j >= 1:
            for i in range(kk):
                if i ^ j > i:
                    exchange(i, i ^ j, True)
            j //= 2
        shift //= 2
    return v


def _peer_select_kernel(ht_ref, wq_ref, keys_ref, c0_ref, e0_ref, a1_ref, e1_ref, top_scr, cand_scr,
                        cs_scr, s_scr, *, lane_chunk):
    hps, _, _, dk = keys_ref.shape
    kk, hk = PEER_TOPK, PEER_TOPK // 2
    q = jnp.dot(wq_ref[...], ht_ref[...], preferred_element_type=F32)
    tt = ht_ref.shape[1]
    for hh in range(hps):
        top, cand, cs = top_scr.at[hh], cand_scr.at[hh], cs_scr.at[hh]
        for z in range(2):
            qz = q[(2 * hh + z) * dk:(2 * hh + z + 1) * dk]
            s_scr[hh, z] = jnp.dot(keys_ref[hh, z], qz, preferred_element_type=F32,
                                   precision=lax.Precision.HIGHEST)
        for lc in range(tt // lane_chunk):
            cols = slice(lc * lane_chunk, (lc + 1) * lane_chunk)
            for z in range(2):
                best_z = _sorted_top(s_scr[hh, z, :, cols], kk)
                for k in range(kk):
                    top[z, k:k + 1, cols] = best_z[k][0:1, :]
            cand[0:kk, cols] = top[0, 0:1, cols] + top[1, :, cols]
            for i in range(1, hk):
                cand[kk + (i - 1) * hk:kk + i * hk, cols] = top[0, i:i + 1, cols] + top[1, 0:hk, cols]
            cand[kk + (hk - 1) * hk:, cols] = top[0, hk:kk, cols] + top[1, 0:1, cols]
            pair_top = _sorted_top(cand[:, cols], kk)
            best = top[0, 0:1, cols] + top[1, 0:1, cols]
            zsum = jnp.zeros_like(best)
            for k in range(kk):
                zsum = zsum + jnp.exp(pair_top[k][0:1, :] - best)
            thr = pair_top[kk - 1][0:1, :]
            a_sorted = top[0, :, cols]
            c_sorted = jnp.full(a_sorted.shape, jnp.inf, F32)
            for j in range(kk):
                bj = top[1, j:j + 1, cols]
                c_sorted = jnp.where(a_sorted + bj >= thr, bj, c_sorted)
            cs[:, cols] = c_sorted
            s0 = s_scr[hh, 0, :, cols]
            s1 = s_scr[hh, 1, :, cols]
            c0 = jnp.full(s0.shape, jnp.inf, F32)
            for i in range(kk):
                c0 = jnp.where(s0 == top[0, i:i + 1, cols], cs[i:i + 1, cols], c0)
            c0_ref[hh, :, cols] = c0
            e0_ref[hh, :, cols] = jnp.exp(s0 - top[0, 0:1, cols]) / zsum
            a1_ref[hh, :, cols] = s1
            e1_ref[hh, :, cols] = jnp.exp(s1 - top[1, 0:1, cols])


def _peer_select_call(ht, wq_t, keys, layer):
    d, t = ht.shape
    heads, _, n_keys, dk = keys.shape[1:]
    tt = _tile(t, 512)
    hps = 2 if heads % 2 == 0 else 1
    sc_spec = pl.BlockSpec((hps, n_keys, tt), lambda i, h: (h, 0, i))
    sc_shape = jax.ShapeDtypeStruct((heads, n_keys, t), F32)
    return pl.pallas_call(
        functools.partial(_peer_select_kernel, lane_chunk=min(tt, 128)),
        grid=(t // tt, heads // hps),
        in_specs=[pl.BlockSpec((d, tt), lambda i, h: (0, i)),
                  pl.BlockSpec((None, hps * 2 * dk, d), lambda i, h: (layer, h, 0)),
                  pl.BlockSpec((None, hps, 2, n_keys, dk), lambda i, h: (layer, h, 0, 0, 0))],
        out_specs=[sc_spec] * 4,
        out_shape=[sc_shape] * 4,
        scratch_shapes=[pltpu.VMEM((hps, 2, PEER_TOPK, tt), F32),
                        pltpu.VMEM((hps, PEER_CAND_ROWS, tt), F32),
                        pltpu.VMEM((hps, PEER_TOPK, tt), F32),
                        pltpu.VMEM((hps, 2, n_keys, tt), F32)],
        compiler_params=_params("arbitrary", "arbitrary"),
        name="peer_select",
    )(ht, wq_t, keys)


def _peer_dense_kernel(ht_ref, u_ref, vt_ref, c0t_ref, e0t_ref, a1_ref, e1_ref, o_ref, act_scr, p_scr,
                       *, lane_chunk):
    nsub, heads, tt = c0t_ref.shape
    n2 = a1_ref.shape[1]
    j = pl.program_id(1)
    slot = j % 2

    @pl.when(j == 0)
    def _():
        o_ref[...] = jnp.zeros_like(o_ref)

    @pl.when(j > 0)
    def _():
        for r in range(nsub):
            rows = slice(r * n2, (r + 1) * n2)
            for lc in range(tt // lane_chunk):
                cols = slice(lc * lane_chunk, (lc + 1) * lane_chunk)
                w = jnp.zeros((n2, lane_chunk), F32)
                for h in range(heads):
                    hit = a1_ref[h, :, cols] >= c0t_ref[r, h:h + 1, cols]
                    w = w + jnp.where(hit, e0t_ref[r, h:h + 1, cols] * e1_ref[h, :, cols], 0.0)
                p_scr[rows, cols] = (w * act_scr[1 - slot, rows, cols]).astype(BF16)
        o_ref[...] += jnp.dot(vt_ref[...], p_scr[...], preferred_element_type=F32)

    @pl.when(j < pl.num_programs(1) - 1)
    def _():
        act_scr[slot] = jax.nn.gelu(jnp.dot(u_ref[...], ht_ref[...], preferred_element_type=F32))


def _peer_dense_call(ht, u_bf, vt_bf, c0t, e0t, a1, e1, layer, *, ec=512):
    d, t = ht.shape
    n_exp = u_bf.shape[1]
    heads, n2, _ = a1.shape
    tt = _tile(t, 512)
    ec = _tile(n_exp, ec)
    nsub = ec // n2
    n_chunks = n_exp // ec
    cur = lambda j: jnp.minimum(j, n_chunks - 1)
    prev = lambda j: jnp.maximum(j - 1, 0)
    row_spec = pl.BlockSpec((nsub, heads, tt), lambda i, j: (prev(j), 0, i))
    once = pl.Buffered(1)
    full_spec = pl.BlockSpec((heads, n2, tt), lambda i, j: (0, 0, i), pipeline_mode=once)
    return pl.pallas_call(
        functools.partial(_peer_dense_kernel, lane_chunk=min(tt, 256)),
        grid=(t // tt, n_chunks + 1),
        in_specs=[pl.BlockSpec((d, tt), lambda i, j: (0, i), pipeline_mode=once),
                  pl.BlockSpec((None, ec, d), lambda i, j: (layer, cur(j), 0)),
                  pl.BlockSpec((None, d, ec), lambda i, j: (layer, 0, prev(j))),
                  row_spec, row_spec, full_spec, full_spec],
        out_specs=pl.BlockSpec((d, tt), lambda i, j: (0, i)),
        out_shape=jax.ShapeDtypeStruct((d, t), F32),
        scratch_shapes=[pltpu.VMEM((2, ec, tt), F32), pltpu.VMEM((ec, tt), BF16)],
        compiler_params=_params("arbitrary", "arbitrary"),
        name="peer_dense",
    )(ht, u_bf, vt_bf, c0t, e0t, a1, e1)


def _peer_ffn_t(ht, wq_t, keys, u_bf, vt_bf, layer):
    c0, e0, a1, e1 = _peer_select_call(ht, wq_t, keys, layer)
    return _peer_dense_call(ht, u_bf, vt_bf, c0.transpose(1, 0, 2), e0.transpose(1, 0, 2), a1, e1, layer)


def kernel(x_prompt, x_sample, c_prompt, c_sample, state_ssm_re, state_ssm_im, ada_w, ada_b, norm_mix_g, norm_ffn_g, final_g, gm_w_in, gm_b_in, gm_v_g, gm_w_s, gm_b_s, gm_w_out, ssm_w_in, ssm_lam_re, ssm_lam_im, ssm_log_dt, ssm_b_re, ssm_b_im, ssm_c_re, ssm_c_im, ssm_d, ssm_w_out, peer_w_q, peer_keys, peer_u, peer_v):
    bp, seq, d = x_prompt.shape
    bs, dseq, _ = x_sample.shape
    depth = ada_w.shape[0]
    tp, ts = bp * seq, bs * dseq
    chunk = gm_w_s.shape[2]
    hw = gm_w_in.shape[2] // 2
    n_groups, n_state = ssm_lam_re.shape[1:]
    w_ssm = ssm_w_in.shape[2]
    assert dseq == ROW_GROUP and seq % chunk == 0 and ts % chunk == 0
    assert bp * 2 == ROW_GROUP and bs % ROW_GROUP == 0
    groups = (bp, seq, bs)

    x = (x_prompt.reshape(tp, d), x_sample.reshape(ts, d))

    m_real = bp + bs
    m_pad = -(-m_real // ROW_GROUP) * ROW_GROUP
    c_all = jnp.pad(jnp.concatenate([c_prompt, c_sample], axis=0), ((0, m_pad - m_real), (0, 0)))
    mods = _ada_call(c_all, ada_w, ada_b)

    gm_w_in_bf, gm_w_out_bf = gm_w_in.astype(BF16), gm_w_out.astype(BF16)
    ssm_w_in_bf, ssm_w_out_bf = ssm_w_in.astype(BF16), ssm_w_out.astype(BF16)
    wq_t = peer_w_q.transpose(0, 2, 1).astype(BF16)
    u_bf = peer_u.astype(BF16)
    vt_bf = peer_v.transpose(0, 2, 1).astype(BF16)

    tril = jnp.tril(jnp.ones((chunk, chunk), F32))
    rep = chunk // dseq
    tril_s = jnp.tril(jnp.ones((dseq, dseq), F32))

    def spatial_operands(ia):
        w_p = gm_w_s[ia] * tril
        w_s = jnp.einsum("ab,gts->gatbs", jnp.eye(rep, dtype=F32),
                         gm_w_s[ia][:, :dseq, :dseq] * tril_s).reshape(-1, chunk, chunk)
        b_p = gm_b_s[ia][:, :, None]
        b_s = jnp.tile(gm_b_s[ia][:, :dseq], (1, rep))[:, :, None]
        return jnp.stack([w_p, w_s]).astype(BF16), jnp.stack([b_p, b_s])

    new_v, st_p, st_s = [], [], []
    peer_t = None
    me_prev = None
    ia = ib = 0
    for i in range(depth):
        me = _group_rows(mods[i], groups)
        if i == 0:
            (h,) = _resnorm_call(x, norm_mix_g[i], groups, me, sc_col=1, sh_col=0)
        else:
            x, h = _resnorm_call(x, norm_mix_g[i], groups, me, delta=peer_t, delta_t=True,
                                 me_gate=me_prev, gate_col=5, sc_col=1, sh_col=0)
        if i % 2 == 0:
            u = _mm_call(h, gm_w_in_bf, ia, bias=gm_b_in, act="gelu", out_dtype=BF16, n_out=hw)
            v_raw = _mm_call(h, gm_w_in_bf, ia, bias=gm_b_in, act="gelu", col_start=hw, n_out=hw)
            wmix, bmix = spatial_operands(ia)
            gated, v_s = _gmlp_gate_call(u, v_raw, gm_v_g[ia], wmix, bmix, tp // chunk)
            new_v.append(v_s.reshape(bs, dseq, hw))
            mix = _mm_call(gated, gm_w_out_bf, ia, tm=768, tn=512)
            ia += 1
        else:
            a_re, a_im, f_re, f_im = _ssm_disc_call(ssm_lam_re[ib], ssm_lam_im[ib], ssm_log_dt[ib])
            bd, cd, a_cat = _ssm_operands(f_re, f_im, a_re, a_im, ssm_b_re[ib], ssm_b_im[ib],
                                          ssm_c_re[ib], ssm_c_im[ib])
            xs_p = _mm_call(h, ssm_w_in_bf, ib, m_rows=tp, seq_major=(bp, seq)).reshape(tp, w_ssm)
            zero = jnp.zeros((bp, n_groups, n_state), F32)
            y_p, hf_p = _ssm_core_call(xs_p, bd, cd, a_cat, ssm_d[ib], _pack_state(zero, zero),
                                       nb=bp, steps=_tile(seq, 256))
            mix_p = _glu_call(y_p.reshape(seq, bp * w_ssm), ssm_w_out_bf, ib, seq_major=(bp, seq))
            xs_s = _mm_call(h, ssm_w_in_bf, ib, row_start=tp, m_rows=ts)
            xs_s = xs_s.reshape(bs, dseq, w_ssm).transpose(1, 0, 2).reshape(ts, w_ssm)
            y_s, hf_s = _ssm_core_call(xs_s, bd, cd, a_cat, ssm_d[ib],
                                       _pack_state(state_ssm_re[ib], state_ssm_im[ib]), nb=bs, steps=dseq)
            y_s = y_s.reshape(dseq, bs, w_ssm).transpose(1, 0, 2).reshape(ts, w_ssm)
            mix = (mix_p, _glu_call(y_s, ssm_w_out_bf, ib))
            st_p.append(_unpack_state(hf_p, bp, n_groups, n_state))
            st_s.append(_unpack_state(hf_s, bs, n_groups, n_state))
            ib += 1
        x, ht = _resnorm_call(x, norm_ffn_g[i], groups, me, delta=mix, me_gate=me, gate_col=2, sc_col=4,
                              sh_col=3, emit_h=False, emit_ht=True)
        peer_t = _peer_ffn_t(ht, wq_t, peer_keys, u_bf, vt_bf, i)
        me_prev = me
    y_p, y_s = _resnorm_call(x, final_g, groups, delta=peer_t, delta_t=True, me_gate=me_prev, gate_col=5,
                             final=True)

    y_prompt = y_p.reshape(bp, seq, d)
    y_sample = y_s.reshape(bs, dseq, d)
    return (y_prompt, y_sample, jnp.stack(new_v),
            jnp.stack([s[0] for s in st_p]), jnp.stack([s[1] for s in st_p]),
            jnp.stack([s[0] for s in st_s]), jnp.stack([s[1] for s in st_s]))
```

```python
import functools
import math

import jax
import jax.numpy as jnp
from jax import lax
from jax.experimental import pallas as pl
from jax.experimental.pallas import tpu as pltpu

EPS = 1e-6
PEER_TOPK = 16
ROW_GROUP = 8
SSM_PACK_GROUPS = 16
VMEM_LIMIT_BYTES = 56 * 1024 * 1024
F32 = jnp.float32
BF16 = jnp.bfloat16


def _params(*sem):
    return pltpu.CompilerParams(dimension_semantics=sem, vmem_limit_bytes=VMEM_LIMIT_BYTES)


def _tile(n, pref, align=128):
    if n <= pref:
        return n
    for cand in range(pref - pref % align, 0, -align):
        if n % cand == 0:
            return cand
    raise ValueError((n, pref, align))


def _ada_kernel(c_ref, w_ref, b_ref, o_ref):
    c = c_ref[...]
    act = (c * (1.0 / (1.0 + jnp.exp(-c)))).astype(BF16)
    o_ref[...] = jnp.dot(act, w_ref[...].astype(BF16), preferred_element_type=F32) + b_ref[...]


def _ada_call(c_all, ada_w, ada_b):
    depth, d, n = ada_w.shape
    m = c_all.shape[0]
    tn = _tile(n, 512)
    return pl.pallas_call(
        _ada_kernel,
        grid=(depth, n // tn),
        in_specs=[pl.BlockSpec((m, d), lambda l, j: (0, 0)),
                  pl.BlockSpec((None, d, tn), lambda l, j: (l, 0, j)),
                  pl.BlockSpec((None, 1, tn), lambda l, j: (l, 0, j))],
        out_specs=pl.BlockSpec((None, m, tn), lambda l, j: (l, 0, j)),
        out_shape=jax.ShapeDtypeStruct((depth, m, n), F32),
        compiler_params=_params("arbitrary", "arbitrary"),
        name="ada_mod",
    )(c_all, ada_w, ada_b.reshape(depth, 1, n))


def _resnorm_kernel(*refs, x_split, has_delta, delta_t, delta_split, emit_h, emit_ht, final,
                    n_prompt_tiles):
    it = iter(refs)
    x_ref = next(it)
    xs_ref = next(it) if x_split else None
    in_prompt = pl.program_id(0) < n_prompt_tiles
    if has_delta:
        delta_ref = next(it)
        delta_s_ref = next(it) if delta_split else None
        gate_ref = next(it)
    g_ref = next(it)
    if not final:
        sc_ref = next(it)
        sh_ref = next(it)
    tm, d = x_ref.shape
    r = tm // ROW_GROUP
    x = x_ref[...]
    if x_split:
        x = jnp.where(in_prompt, x, xs_ref[...])
    x3 = x.reshape(r, ROW_GROUP, d)
    if has_delta:
        delta = delta_ref[...]
        if delta_split:
            delta = jnp.where(in_prompt, delta, delta_s_ref[...])
        if delta_t:
            delta = delta.T
        x3 = x3 + gate_ref[...][:, None, :] * delta.reshape(r, ROW_GROUP, d)
    var = jnp.mean(x3 * x3, axis=-1, keepdims=True)
    nrm = x3 * lax.rsqrt(var + EPS) * g_ref[...]
    if final:
        yp_ref, ys_ref = next(it), next(it)

        @pl.when(in_prompt)
        def _():
            yp_ref[...] = nrm.reshape(tm, d)

        @pl.when(jnp.logical_not(in_prompt))
        def _():
            ys_ref[...] = nrm.reshape(tm, d)

        return
    h = (nrm * (1.0 + sc_ref[...][:, None, :]) + sh_ref[...][:, None, :]).reshape(tm, d)
    if has_delta:
        next(it)[...] = x3.reshape(tm, d)
    if emit_h:
        next(it)[...] = h.astype(BF16)
    if emit_ht:
        next(it)[...] = h.T.astype(BF16)


def _resnorm_tile(groups):
    bp, seq, bs = groups
    return _tile(math.gcd(seq, bs * ROW_GROUP), 256)


def _group_rows(mod, groups):
    bp, _, bs = groups
    r = _resnorm_tile(groups) // ROW_GROUP
    return jnp.concatenate([jnp.repeat(mod[:bp], r, axis=0), mod[bp:bp + bs]], axis=0)


def _resnorm_call(x, gain, groups, me=None, *, delta=None, delta_t=False, me_gate=None, gate_col=None,
                  sc_col=None, sh_col=None, emit_h=True, emit_ht=False, final=False):
    x_split = isinstance(x, tuple)
    d = x[0].shape[1] if x_split else x.shape[1]
    bp, seq, bs = groups
    t = bp * seq + bs * ROW_GROUP
    tm = _resnorm_tile(groups)
    r = tm // ROW_GROUP
    tps = seq // tm
    n_prompt_tiles = bp * tps
    has_delta = delta is not None
    row = pl.BlockSpec((tm, d), lambda i: (i, 0))
    col = pl.BlockSpec((d, tm), lambda i: (0, i))

    def me_spec(k):
        return pl.BlockSpec(
            (r, d), lambda i, k=k: (jnp.where(i < n_prompt_tiles, i // tps, bp + i - n_prompt_tiles), k))

    row_p = pl.BlockSpec((tm, d), lambda i: (jnp.minimum(i, n_prompt_tiles - 1), 0))
    row_s = pl.BlockSpec((tm, d), lambda i: (jnp.maximum(i - n_prompt_tiles, 0), 0))
    delta_split = isinstance(delta, tuple)
    in_specs, args = ([row_p, row_s], list(x)) if x_split else ([row], [x])
    if has_delta:
        if delta_split:
            assert not delta_t
            in_specs += [row_p, row_s]
            args += list(delta)
        else:
            in_specs.append(col if delta_t else row)
            args.append(delta)
        in_specs.append(me_spec(gate_col))
        args.append(me_gate)
    in_specs.append(pl.BlockSpec((1, d), lambda i: (0, 0)))
    args.append(gain.reshape(1, d))
    if not final:
        in_specs += [me_spec(sc_col), me_spec(sh_col)]
        args += [me, me]
    out_specs, out_shape = [], []
    if final:
        out_specs += [row_p, row_s]
        out_shape += [jax.ShapeDtypeStruct((n_prompt_tiles * tm, d), F32),
                      jax.ShapeDtypeStruct((t - n_prompt_tiles * tm, d), F32)]
    elif has_delta:
        out_specs.append(row)
        out_shape.append(jax.ShapeDtypeStruct((t, d), F32))
    if not final and emit_h:
        out_specs.append(row)
        out_shape.append(jax.ShapeDtypeStruct((t, d), BF16))
    if not final and emit_ht:
        out_specs.append(col)
        out_shape.append(jax.ShapeDtypeStruct((d, t), BF16))
    return pl.pallas_call(
        functools.partial(_resnorm_kernel, x_split=x_split, has_delta=has_delta, delta_t=delta_t,
                          delta_split=delta_split,
                          emit_h=emit_h, emit_ht=emit_ht, final=final, n_prompt_tiles=n_prompt_tiles),
        grid=(t // tm,),
        in_specs=in_specs, out_specs=out_specs, out_shape=out_shape,
        compiler_params=_params("arbitrary"),
        name="resnorm",
    )(*args)


def _mm_kernel(a_ref, b_ref, *rest, has_bias, act, row_block=512):
    o_ref = rest[-1]
    tm = a_ref.shape[0]
    rb = row_block if (has_bias or act) and tm % row_block == 0 else tm
    for r0 in range(0, tm, rb):
        acc = jnp.dot(a_ref[r0:r0 + rb, :], b_ref[...], preferred_element_type=F32)
        if has_bias:
            acc = acc + rest[0][...]
        if act == "gelu":
            acc = jax.nn.gelu(acc)
        o_ref[r0:r0 + rb, :] = acc.astype(o_ref.dtype)


def _mm_call(a, b, layer, *, bias=None, act=None, out_dtype=F32, col_start=0, n_out=None,
             row_start=0, m_rows=None, seq_major=None, tm=1536, tn=512):
    k = a.shape[1]
    m = a.shape[0] if m_rows is None else m_rows
    n_out = b.shape[2] if n_out is None else n_out
    tm = _tile(m if seq_major is None else seq_major[1], tm)
    tn = _tile(n_out, tn)
    assert col_start % tn == 0 and row_start % tm == 0
    off, roff, nj = col_start // tn, row_start // tm, n_out // tn
    in_specs = [pl.BlockSpec((tm, k), lambda i, j: (i + roff, 0)),
                pl.BlockSpec((None, k, tn), lambda i, j: (layer, 0, j + off))]
    args = [a, b]
    if bias is not None:
        in_specs.append(pl.BlockSpec((None, 1, tn), lambda i, j: (layer, 0, j + off)))
        args.append(bias.reshape(bias.shape[0], 1, -1))
    if seq_major is None:
        out_spec = pl.BlockSpec((tm, tn), lambda i, j: (i, j))
        out_shape = jax.ShapeDtypeStruct((m, n_out), out_dtype)
    else:
        n_seq, seq_len = seq_major
        tps = seq_len // tm
        out_spec = pl.BlockSpec((tm, tn), lambda i, j: (i % tps, (i // tps) * nj + j))
        out_shape = jax.ShapeDtypeStruct((seq_len, n_seq * n_out), out_dtype)
    return pl.pallas_call(
        functools.partial(_mm_kernel, has_bias=bias is not None, act=act),
        grid=(m // tm, nj),
        in_specs=in_specs,
        out_specs=out_spec,
        out_shape=out_shape,
        compiler_params=_params("arbitrary", "arbitrary"),
        name="matmul",
    )(*args)


def _glu_kernel(a_ref, ba_ref, bg_ref, o_ref):
    a = a_ref[...]
    lin = jnp.dot(a, ba_ref[...], preferred_element_type=F32)
    gat = jnp.dot(a, bg_ref[...], preferred_element_type=F32)
    o_ref[...] = lin * (1.0 / (1.0 + jnp.exp(-gat)))


def _glu_call(a, b, layer, *, seq_major=None, tm=1024, tn=256):
    k = b.shape[1]
    n = b.shape[2] // 2
    if seq_major is None:
        m = a.shape[0]
        tm = _tile(m, tm)
        a_spec = pl.BlockSpec((tm, k), lambda i, j: (i, 0))
    else:
        n_seq, seq_len = seq_major
        m = n_seq * seq_len
        tm = _tile(seq_len, tm)
        tps = seq_len // tm
        a_spec = pl.BlockSpec((tm, k), lambda i, j: (i % tps, i // tps))
    tn = _tile(n, tn)
    off = n // tn
    return pl.pallas_call(
        _glu_kernel,
        grid=(m // tm, n // tn),
        in_specs=[a_spec,
                  pl.BlockSpec((None, k, tn), lambda i, j: (layer, 0, j)),
                  pl.BlockSpec((None, k, tn), lambda i, j: (layer, 0, j + off))],
        out_specs=pl.BlockSpec((tm, tn), lambda i, j: (i, j)),
        out_shape=jax.ShapeDtypeStruct((m, n), F32),
        compiler_params=_params("arbitrary", "arbitrary"),
        name="glu_matmul",
    )(a, b, b)


def _gmlp_gate_kernel(u_ref, v_ref, vg_ref, w_ref, b_ref, o_ref, vs_ref, *, groups, n_prompt_tiles):
    v = v_ref[...]
    var = jnp.mean(v * v, axis=-1, keepdims=True)
    vn = v * lax.rsqrt(var + EPS) * vg_ref[...]

    @pl.when(pl.program_id(0) >= n_prompt_tiles)
    def _():
        vs_ref[...] = vn

    gdim = v.shape[1] // groups
    for g in range(groups):
        sl = slice(g * gdim, (g + 1) * gdim)
        s = jnp.dot(w_ref[g], vn[:, sl].astype(BF16), preferred_element_type=F32) + b_ref[g]
        o_ref[:, sl] = (u_ref[:, sl].astype(F32) * s).astype(BF16)


def _gmlp_gate_call(u, v_raw, v_g, wmix, bmix, n_prompt_tiles):
    t, hw = u.shape
    _, groups, blk, _ = wmix.shape
    n_tiles = t // blk
    kind = lambda i: jnp.where(i >= n_prompt_tiles, 1, 0)
    return pl.pallas_call(
        functools.partial(_gmlp_gate_kernel, groups=groups, n_prompt_tiles=n_prompt_tiles),
        grid=(n_tiles,),
        in_specs=[pl.BlockSpec((blk, hw), lambda i: (i, 0)),
                  pl.BlockSpec((blk, hw), lambda i: (i, 0)),
                  pl.BlockSpec((1, hw), lambda i: (0, 0)),
                  pl.BlockSpec((None, groups, blk, blk), lambda i: (kind(i), 0, 0, 0)),
                  pl.BlockSpec((None, groups, blk, 1), lambda i: (kind(i), 0, 0, 0))],
        out_specs=[pl.BlockSpec((blk, hw), lambda i: (i, 0)),
                   pl.BlockSpec((blk, hw), lambda i: (jnp.maximum(i - n_prompt_tiles, 0), 0))],
        out_shape=[jax.ShapeDtypeStruct((t, hw), BF16),
                   jax.ShapeDtypeStruct(((n_tiles - n_prompt_tiles) * blk, hw), F32)],
        compiler_params=_params("arbitrary"),
        name="gmlp_gate",
    )(u, v_raw, v_g.reshape(1, hw), wmix, bmix)


def _ssm_disc_kernel(lre_ref, lim_ref, ldt_ref, are_ref, aim_ref, fre_ref, fim_ref):
    lre = lre_ref[...]
    lim = lim_ref[...]
    dt = jnp.exp(ldt_ref[...])
    mag = jnp.exp(lre * dt)
    ang = lim * dt
    a_re = mag * jnp.cos(ang)
    a_im = mag * jnp.sin(ang)
    n_re = a_re - 1
    den = lre * lre + lim * lim
    are_ref[...] = a_re
    aim_ref[...] = a_im
    fre_ref[...] = (n_re * lre + a_im * lim) / den
    fim_ref[...] = (a_im * lre - n_re * lim) / den


def _ssm_disc_call(lam_re, lam_im, log_dt):
    g, p = lam_re.shape
    shp = jax.ShapeDtypeStruct((g, p), F32)
    return pl.pallas_call(_ssm_disc_kernel, out_shape=[shp] * 4, name="ssm_discretize")(
        lam_re, lam_im, log_dt.reshape(g, 1))


def _ssm_core_kernel(xs_ref, bd_ref, cd_ref, a_ref, d_ref, h0_ref, y_ref, hf_ref, bu_scr, st_scr,
                     *, nb, col_chunk):
    n_rows, ch = xs_ref.shape
    ns = a_ref.shape[1] // 2
    half = ROW_GROUP // 2

    @pl.when(pl.program_id(1) == 0)
    def _():
        st_scr[...] = h0_ref[...]

    xs = xs_ref[...]
    xs_bf = xs.astype(BF16)
    y = d_ref[...] * xs
    if nb % ROW_GROUP != 0:
        assert nb == half
        hi = lax.broadcasted_iota(jnp.int32, (ROW_GROUP, col_chunk), 0) >= half

    for cc in range(ns // col_chunk):
        cre = slice(cc * col_chunk, (cc + 1) * col_chunk)
        cim = slice(ns + cc * col_chunk, ns + (cc + 1) * col_chunk)
        a_re = jnp.broadcast_to(a_ref[:, cre], (ROW_GROUP, col_chunk))
        a_im = jnp.broadcast_to(a_ref[:, cim], (ROW_GROUP, col_chunk))
        bu_scr[:, cre] = jnp.dot(xs_bf, bd_ref[:, cre], preferred_element_type=F32)
        bu_scr[:, cim] = jnp.dot(xs_bf, bd_ref[:, cim], preferred_element_type=F32)

        if nb % ROW_GROUP == 0:
            for rb in range(nb // ROW_GROUP):
                srows = slice(rb * ROW_GROUP, (rb + 1) * ROW_GROUP)
                s_re, s_im = st_scr[srows, cre], st_scr[srows, cim]
                for l in range(n_rows // nb):
                    rows = slice(l * nb + rb * ROW_GROUP, l * nb + (rb + 1) * ROW_GROUP)
                    s_re, s_im = (a_re * s_re - a_im * s_im + bu_scr[rows, cre],
                                  a_re * s_im + a_im * s_re + bu_scr[rows, cim])
                    bu_scr[rows, cre] = s_re
                    bu_scr[rows, cim] = s_im
                st_scr[srows, cre] = s_re
                st_scr[srows, cim] = s_im
        else:
            ahi_re = jnp.where(hi, a_re, 0.0)
            ahi_im = jnp.where(hi, a_im, 0.0)
            ap_re = jnp.where(hi, a_re * a_re - a_im * a_im, a_re)
            ap_im = jnp.where(hi, 2.0 * a_re * a_im, a_im)
            c_re, c_im = st_scr[:, cre], st_scr[:, cim]
            for s in range(n_rows // ROW_GROUP):
                rows = slice(s * ROW_GROUP, (s + 1) * ROW_GROUP)
                v_re = bu_scr[rows, cre]
                v_im = bu_scr[rows, cim]
                r_re = pltpu.roll(v_re, half, 0)
                r_im = pltpu.roll(v_im, half, 0)
                o_re = ap_re * c_re - ap_im * c_im + (v_re + (ahi_re * r_re - ahi_im * r_im))
                o_im = ap_re * c_im + ap_im * c_re + (v_im + (ahi_re * r_im + ahi_im * r_re))
                bu_scr[rows, cre] = o_re
                bu_scr[rows, cim] = o_im
                c_re = jnp.where(hi, o_re, pltpu.roll(o_re, half, 0))
                c_im = jnp.where(hi, o_im, pltpu.roll(o_im, half, 0))
            st_scr[:, cre] = c_re
            st_scr[:, cim] = c_im

        y = y + jnp.dot(bu_scr[:, cre].astype(BF16), cd_ref[cre, :], preferred_element_type=F32)
        y = y + jnp.dot(bu_scr[:, cim].astype(BF16), cd_ref[cim, :], preferred_element_type=F32)
    y_ref[...] = jax.nn.gelu(y).astype(BF16)
    hf_ref[...] = st_scr[...]


def _ssm_core_call(xs, bd, cd, a_cat, d_skip, h0, *, nb, steps):
    n_rows, w = xs.shape
    packs, kb, ns2 = bd.shape
    ch = cd.shape[2]
    nst = h0.shape[1]
    tr = steps * nb
    return pl.pallas_call(
        functools.partial(_ssm_core_kernel, nb=nb, col_chunk=min(ns2 // 2, 256)),
        grid=(packs, n_rows // tr),
        in_specs=[pl.BlockSpec((tr, ch), lambda c, t: (t, c)),
                  pl.BlockSpec((None, kb, ns2), lambda c, t: (c, 0, 0)),
                  pl.BlockSpec((None, ns2, ch), lambda c, t: (c, 0, 0)),
                  pl.BlockSpec((None, 1, ns2), lambda c, t: (c, 0, 0)),
                  pl.BlockSpec((1, ch), lambda c, t: (0, c)),
                  pl.BlockSpec((None, nst, ns2), lambda c, t: (c, 0, 0))],
        out_specs=[pl.BlockSpec((tr, ch), lambda c, t: (t, c)),
                   pl.BlockSpec((None, nst, ns2), lambda c, t: (c, 0, 0))],
        out_shape=[jax.ShapeDtypeStruct((n_rows, w), BF16),
                   jax.ShapeDtypeStruct((packs, nst, ns2), F32)],
        scratch_shapes=[pltpu.VMEM((tr, ns2), F32), pltpu.VMEM((nst, ns2), F32)],
        compiler_params=_params("arbitrary", "arbitrary"),
        name="ssm_core",
    )(xs, bd, cd, a_cat, d_skip.reshape(1, w), h0)


def _ssm_operands(f_re, f_im, a_re, a_im, b_re, b_im, c_re, c_im):
    g, p, k = b_re.shape
    pg = SSM_PACK_GROUPS
    packs = g // pg
    bb_re = f_re[..., None] * b_re - f_im[..., None] * b_im
    bb_im = f_re[..., None] * b_im + f_im[..., None] * b_re
    eye = jnp.eye(pg, dtype=F32)

    def bdiag(m):
        m = m.reshape(packs, pg, p, k).transpose(0, 1, 3, 2)
        return (m[:, :, :, None, :] * eye[None, :, None, :, None]).reshape(packs, pg * k, pg * p)

    def cdiag(m):
        m = m.reshape(packs, pg, k, p).transpose(0, 1, 3, 2)
        return (m[:, :, :, None, :] * eye[None, :, None, :, None]).reshape(packs, pg * p, pg * k)

    bd = jnp.concatenate([bdiag(bb_re), bdiag(bb_im)], axis=2).astype(BF16)
    cd = jnp.concatenate([cdiag(c_re), cdiag(-c_im)], axis=1).astype(BF16)
    a_cat = jnp.concatenate([a_re.reshape(packs, 1, pg * p), a_im.reshape(packs, 1, pg * p)], axis=2)
    return bd, cd, a_cat


def _pack_state(s_re, s_im):
    b, g, p = s_re.shape
    packs = g // SSM_PACK_GROUPS

    def one(s):
        s = s.reshape(b, packs, SSM_PACK_GROUPS * p).transpose(1, 0, 2)
        return s if b % ROW_GROUP == 0 else jnp.tile(s, (1, ROW_GROUP // b, 1))

    return jnp.concatenate([one(s_re), one(s_im)], axis=2)


def _unpack_state(hf, b, g, p):
    ns = hf.shape[2] // 2

    def one(s):
        return s[:, :b].transpose(1, 0, 2).reshape(b, g, p)

    return one(hf[:, :, :ns]), one(hf[:, :, ns:])


PEER_CAND_ROWS = PEER_TOPK + (PEER_TOPK // 2 - 1) * (PEER_TOPK // 2) + PEER_TOPK // 2


def _sorted_top(s, kk):
    n, lanes = s.shape
    groups = n // ROW_GROUP
    assert n % ROW_GROUP == 0 and groups <= kk and kk & (kk - 1) == 0
    v = [s[i * ROW_GROUP:(i + 1) * ROW_GROUP] for i in range(groups)]
    v += [jnp.full((ROW_GROUP, lanes), -jnp.inf, F32)] * (kk - groups)

    def exchange(i, l, descending):
        big, small = jnp.maximum(v[i], v[l]), jnp.minimum(v[i], v[l])
        v[i], v[l] = (big, small) if descending else (small, big)

    k = 2
    while k <= kk:
        j = k // 2
        while j >= 1:
            for i in range(kk):
                if i ^ j > i:
                    exchange(i, i ^ j, (i & k) == 0)
            j //= 2
        k *= 2
    shift = ROW_GROUP // 2
    while shift >= 1:
        other = [pltpu.roll(v[kk - 1 - i], shift, 0) for i in range(kk)]
        v = [jnp.maximum(v[i], other[i]) for i in range(kk)]
        j = kk // 2
        while j >= 1:
            for i in range(kk):
                if i ^ j > i:
                    exchange(i, i ^ j, True)
            j //= 2
        shift //= 2
    return v


def _peer_select_kernel(ht_ref, wq_ref, keys_ref, c0_ref, e0_ref, a1_ref, e1_ref, top_scr, cand_scr,
                        cs_scr, s_scr, *, lane_chunk):
    hps, _, _, dk = keys_ref.shape
    kk, hk = PEER_TOPK, PEER_TOPK // 2
    q = jnp.dot(wq_ref[...], ht_ref[...], preferred_element_type=F32)
    tt = ht_ref.shape[1]
    for hh in range(hps):
        top, cand, cs = top_scr.at[hh], cand_scr.at[hh], cs_scr.at[hh]
        for z in range(2):
            qz = q[(2 * hh + z) * dk:(2 * hh + z + 1) * dk]
            s_scr[hh, z] = jnp.dot(keys_ref[hh, z], qz, preferred_element_type=F32,
                                   precision=lax.Precision.HIGHEST)
        for lc in range(tt // lane_chunk):
            cols = slice(lc * lane_chunk, (lc + 1) * lane_chunk)
            for z in range(2):
                best_z = _sorted_top(s_scr[hh, z, :, cols], kk)
                for k in range(kk):
                    top[z, k:k + 1, cols] = best_z[k][0:1, :]
            cand[0:kk, cols] = top[0, 0:1, cols] + top[1, :, cols]
            for i in range(1, hk):
                cand[kk + (i - 1) * hk:kk + i * hk, cols] = top[0, i:i + 1, cols] + top[1, 0:hk, cols]
            cand[kk + (hk - 1) * hk:, cols] = top[0, hk:kk, cols] + top[1, 0:1, cols]
            pair_top = _sorted_top(cand[:, cols], kk)
            best = top[0, 0:1, cols] + top[1, 0:1, cols]
            zsum = jnp.zeros_like(best)
            for k in range(kk):
                zsum = zsum + jnp.exp(pair_top[k][0:1, :] - best)
            thr = pair_top[kk - 1][0:1, :]
            a_sorted = top[0, :, cols]
            c_sorted = jnp.full(a_sorted.shape, jnp.inf, F32)
            for j in range(kk):
                bj = top[1, j:j + 1, cols]
                c_sorted = jnp.where(a_sorted + bj >= thr, bj, c_sorted)
            cs[:, cols] = c_sorted
            s0 = s_scr[hh, 0, :, cols]
            s1 = s_scr[hh, 1, :, cols]
            c0 = jnp.full(s0.shape, jnp.inf, F32)
            for i in range(kk):
                c0 = jnp.where(s0 == top[0, i:i + 1, cols], cs[i:i + 1, cols], c0)
            c0_ref[hh, :, cols] = c0
            e0_ref[hh, :, cols] = jnp.exp(s0 - top[0, 0:1, cols]) / zsum
            a1_ref[hh, :, cols] = s1
            e1_ref[hh, :, cols] = jnp.exp(s1 - top[1, 0:1, cols])


def _peer_select_call(ht, wq_t, keys, layer):
    d, t = ht.shape
    heads, _, n_keys, dk = keys.shape[1:]
    tt = _tile(t, 512)
    hps = next(k for k in (4, 2, 1) if heads % k == 0)
    sc_spec = pl.BlockSpec((hps, n_keys, tt), lambda i, h: (h, 0, i))
    sc_shape = jax.ShapeDtypeStruct((heads, n_keys, t), F32)
    return pl.pallas_call(
        functools.partial(_peer_select_kernel, lane_chunk=min(tt, 128)),
        grid=(t // tt, heads // hps),
        in_specs=[pl.BlockSpec((d, tt), lambda i, h: (0, i)),
                  pl.BlockSpec((None, hps * 2 * dk, d), lambda i, h: (layer, h, 0)),
                  pl.BlockSpec((None, hps, 2, n_keys, dk), lambda i, h: (layer, h, 0, 0, 0))],
        out_specs=[sc_spec] * 4,
        out_shape=[sc_shape] * 4,
        scratch_shapes=[pltpu.VMEM((hps, 2, PEER_TOPK, tt), F32),
                        pltpu.VMEM((hps, PEER_CAND_ROWS, tt), F32),
                        pltpu.VMEM((hps, PEER_TOPK, tt), F32),
                        pltpu.VMEM((hps, 2, n_keys, tt), F32)],
        compiler_params=_params("arbitrary", "arbitrary"),
        name="peer_select",
    )(ht, wq_t, keys)


def _peer_dense_kernel(ht_ref, u_ref, vt_ref, c0t_ref, e0t_ref, a1_ref, e1_ref, o_ref, act_scr, p_scr,
                       *, lane_chunk):
    nsub, heads, tt = c0t_ref.shape
    n2 = a1_ref.shape[1]
    j = pl.program_id(1)
    slot = j % 2

    @pl.when(j == 0)
    def _():
        o_ref[...] = jnp.zeros_like(o_ref)

    @pl.when(j > 0)
    def _():
        for r in range(nsub):
            rows = slice(r * n2, (r + 1) * n2)
            for lc in range(tt // lane_chunk):
                cols = slice(lc * lane_chunk, (lc + 1) * lane_chunk)
                w = jnp.zeros((n2, lane_chunk), F32)
                for h in range(heads):
                    hit = a1_ref[h, :, cols] >= c0t_ref[r, h:h + 1, cols]
                    w = w + jnp.where(hit, e0t_ref[r, h:h + 1, cols] * e1_ref[h, :, cols], 0.0)
                p_scr[rows, cols] = (w * act_scr[1 - slot, rows, cols]).astype(BF16)
        o_ref[...] += jnp.dot(vt_ref[...], p_scr[...], preferred_element_type=F32)

    @pl.when(j < pl.num_programs(1) - 1)
    def _():
        act_scr[slot] = jax.nn.gelu(jnp.dot(u_ref[...], ht_ref[...], preferred_element_type=F32))


def _peer_dense_call(ht, u_bf, vt_bf, c0t, e0t, a1, e1, layer, *, ec=512):
    d, t = ht.shape
    n_exp = u_bf.shape[1]
    heads, n2, _ = a1.shape
    tt = _tile(t, 512)
    ec = _tile(n_exp, ec)
    nsub = ec // n2
    n_chunks = n_exp // ec
    cur = lambda j: jnp.minimum(j, n_chunks - 1)
    prev = lambda j: jnp.maximum(j - 1, 0)
    row_spec = pl.BlockSpec((nsub, heads, tt), lambda i, j: (prev(j), 0, i))
    once = pl.Buffered(1)
    full_spec = pl.BlockSpec((heads, n2, tt), lambda i, j: (0, 0, i), pipeline_mode=once)
    return pl.pallas_call(
        functools.partial(_peer_dense_kernel, lane_chunk=min(tt, 256)),
        grid=(t // tt, n_chunks + 1),
        in_specs=[pl.BlockSpec((d, tt), lambda i, j: (0, i), pipeline_mode=once),
                  pl.BlockSpec((None, ec, d), lambda i, j: (layer, cur(j), 0)),
                  pl.BlockSpec((None, d, ec), lambda i, j: (layer, 0, prev(j))),
                  row_spec, row_spec, full_spec, full_spec],
        out_specs=pl.BlockSpec((d, tt), lambda i, j: (0, i)),
        out_shape=jax.ShapeDtypeStruct((d, t), F32),
        scratch_shapes=[pltpu.VMEM((2, ec, tt), F32), pltpu.VMEM((ec, tt), BF16)],
        compiler_params=_params("arbitrary", "arbitrary"),
        name="peer_dense",
    )(ht, u_bf, vt_bf, c0t, e0t, a1, e1)


def _peer_ffn_t(ht, wq_t, keys, u_bf, vt_bf, layer):
    c0, e0, a1, e1 = _peer_select_call(ht, wq_t, keys, layer)
    return _peer_dense_call(ht, u_bf, vt_bf, c0.transpose(1, 0, 2), e0.transpose(1, 0, 2), a1, e1, layer)


def kernel(x_prompt, x_sample, c_prompt, c_sample, state_ssm_re, state_ssm_im, ada_w, ada_b, norm_mix_g, norm_ffn_g, final_g, gm_w_in, gm_b_in, gm_v_g, gm_w_s, gm_b_s, gm_w_out, ssm_w_in, ssm_lam_re, ssm_lam_im, ssm_log_dt, ssm_b_re, ssm_b_im, ssm_c_re, ssm_c_im, ssm_d, ssm_w_out, peer_w_q, peer_keys, peer_u, peer_v):
    bp, seq, d = x_prompt.shape
    bs, dseq, _ = x_sample.shape
    depth = ada_w.shape[0]
    tp, ts = bp * seq, bs * dseq
    chunk = gm_w_s.shape[2]
    hw = gm_w_in.shape[2] // 2
    n_groups, n_state = ssm_lam_re.shape[1:]
    w_ssm = ssm_w_in.shape[2]
    assert dseq == ROW_GROUP and seq % chunk == 0 and ts % chunk == 0
    assert bp * 2 == ROW_GROUP and bs % ROW_GROUP == 0
    groups = (bp, seq, bs)

    x = (x_prompt.reshape(tp, d), x_sample.reshape(ts, d))

    m_real = bp + bs
    m_pad = -(-m_real // ROW_GROUP) * ROW_GROUP
    c_all = jnp.pad(jnp.concatenate([c_prompt, c_sample], axis=0), ((0, m_pad - m_real), (0, 0)))
    mods = _ada_call(c_all, ada_w, ada_b)

    gm_w_in_bf, gm_w_out_bf = gm_w_in.astype(BF16), gm_w_out.astype(BF16)
    ssm_w_in_bf, ssm_w_out_bf = ssm_w_in.astype(BF16), ssm_w_out.astype(BF16)
    wq_t = peer_w_q.transpose(0, 2, 1).astype(BF16)
    u_bf = peer_u.astype(BF16)
    vt_bf = peer_v.transpose(0, 2, 1).astype(BF16)

    tril = jnp.tril(jnp.ones((chunk, chunk), F32))
    rep = chunk // dseq
    tril_s = jnp.tril(jnp.ones((dseq, dseq), F32))

    def spatial_operands(ia):
        w_p = gm_w_s[ia] * tril
        w_s = jnp.einsum("ab,gts->gatbs", jnp.eye(rep, dtype=F32),
                         gm_w_s[ia][:, :dseq, :dseq] * tril_s).reshape(-1, chunk, chunk)
        b_p = gm_b_s[ia][:, :, None]
        b_s = jnp.tile(gm_b_s[ia][:, :dseq], (1, rep))[:, :, None]
        return jnp.stack([w_p, w_s]).astype(BF16), jnp.stack([b_p, b_s])

    new_v, st_p, st_s = [], [], []
    peer_t = None
    me_prev = None
    ia = ib = 0
    for i in range(depth):
        me = _group_rows(mods[i], groups)
        if i == 0:
            (h,) = _resnorm_call(x, norm_mix_g[i], groups, me, sc_col=1, sh_col=0)
        else:
            x, h = _resnorm_call(x, norm_mix_g[i], groups, me, delta=peer_t, delta_t=True,
                                 me_gate=me_prev, gate_col=5, sc_col=1, sh_col=0)
        if i % 2 == 0:
            u = _mm_call(h, gm_w_in_bf, ia, bias=gm_b_in, act="gelu", out_dtype=BF16, n_out=hw)
            v_raw = _mm_call(h, gm_w_in_bf, ia, bias=gm_b_in, act="gelu", col_start=hw, n_out=hw)
            wmix, bmix = spatial_operands(ia)
            gated, v_s = _gmlp_gate_call(u, v_raw, gm_v_g[ia], wmix, bmix, tp // chunk)
            new_v.append(v_s.reshape(bs, dseq, hw))
            mix = _mm_call(gated, gm_w_out_bf, ia, tm=768, tn=512)
            ia += 1
        else:
            a_re, a_im, f_re, f_im = _ssm_disc_call(ssm_lam_re[ib], ssm_lam_im[ib], ssm_log_dt[ib])
            bd, cd, a_cat = _ssm_operands(f_re, f_im, a_re, a_im, ssm_b_re[ib], ssm_b_im[ib],
                                          ssm_c_re[ib], ssm_c_im[ib])
            xs_p = _mm_call(h, ssm_w_in_bf, ib, m_rows=tp, seq_major=(bp, seq)).reshape(tp, w_ssm)
            zero = jnp.zeros((bp, n_groups, n_state), F32)
            y_p, hf_p = _ssm_core_call(xs_p, bd, cd, a_cat, ssm_d[ib], _pack_state(zero, zero),
                                       nb=bp, steps=_tile(seq, 256))
            mix_p = _glu_call(y_p.reshape(seq, bp * w_ssm), ssm_w_out_bf, ib, seq_major=(bp, seq))
            xs_s = _mm_call(h, ssm_w_in_bf, ib, row_start=tp, m_rows=ts)
            xs_s = xs_s.reshape(bs, dseq, w_ssm).transpose(1, 0, 2).reshape(ts, w_ssm)
            y_s, hf_s = _ssm_core_call(xs_s, bd, cd, a_cat, ssm_d[ib],
                                       _pack_state(state_ssm_re[ib], state_ssm_im[ib]), nb=bs, steps=dseq)
            y_s = y_s.reshape(dseq, bs, w_ssm).transpose(1, 0, 2).reshape(ts, w_ssm)
            mix = (mix_p, _glu_call(y_s, ssm_w_out_bf, ib))
            st_p.append(_unpack_state(hf_p, bp, n_groups, n_state))
            st_s.append(_unpack_state(hf_s, bs, n_groups, n_state))
            ib += 1
        x, ht = _resnorm_call(x, norm_ffn_g[i], groups, me, delta=mix, me_gate=me, gate_col=2, sc_col=4,
                              sh_col=3, emit_h=False, emit_ht=True)
        peer_t = _peer_ffn_t(ht, wq_t, peer_keys, u_bf, vt_bf, i)
        me_prev = me
    y_p, y_s = _resnorm_call(x, final_g, groups, delta=peer_t, delta_t=True, me_gate=me_prev, gate_col=5,
                             final=True)

    y_prompt = y_p.reshape(bp, seq, d)
    y_sample = y_s.reshape(bs, dseq, d)
    return (y_prompt, y_sample, jnp.stack(new_v),
            jnp.stack([s[0] for s in st_p]), jnp.stack([s[1] for s in st_p]),
            jnp.stack([s[0] for s in st_s]), jnp.stack([s[1] for s in st_s]))
```
